```python
import math
import jax, jax.numpy as jnp
from jax import lax
import numpy as np

D_MODEL = 4096
BATCH = 32
SEQ = 256
DEPTH = 4
DEC_BATCH = 2
DEC_SEQ = 2048
PAST_LEN = 256

GRID_W = 64
POOL_WIDTH = 1024
POOL_WINDOWS = (2, 4, 8, 16)
N_POOL = len(POOL_WINDOWS)
POOL_GROUP = POOL_WIDTH // N_POOL
SSD_INNER = 2048
SSD_HEADDIM = 64
SSD_HEADS = SSD_INNER // SSD_HEADDIM
SSD_GROUPS = 4
SSD_STATE = 128
SSD_CONV = 4
SSD_CHUNK = 128
SSD_XBC = SSD_INNER + 2 * SSD_GROUPS * SSD_STATE
NA_HEADS = 16
NA_HEADDIM = 64
NA_WIDTH = NA_HEADS * NA_HEADDIM
NA_ROWS = 8
NA_COLS = 16
NA_QCOLS = 16
NA_KCOLS = NA_QCOLS + NA_COLS
ATTN_QBLOCK = 128
CONF_WIDTH = 1024
CONF_KERNEL = 31
N_BRANCH = 4
IN_SIZES = (POOL_WIDTH, SSD_INNER, SSD_XBC, 2 * SSD_HEADS, 3 * NA_WIDTH, 2 * CONF_WIDTH, N_BRANCH * D_MODEL)
IN_WIDTH = sum(IN_SIZES)
IN_OFFSETS = tuple(sum(IN_SIZES[:i + 1]) for i in range(len(IN_SIZES) - 1))
N_EXPERTS = 16
EXPERT_FF = 1024
EC_CAPACITY = 2
DEEPNORM_ALPHA = (2 * DEPTH) ** 0.25
DEEPNORM_BETA = (8 * DEPTH) ** -0.25
LN_EPS = 1e-5

kernel_name = 'hybrid_flow_trunk_step'


def layer_norm(x, g, b):
    xf = x.astype(jnp.float32)
    mu = jnp.mean(xf, axis=-1, keepdims=True)
    var = jnp.mean(jnp.square(xf - mu), axis=-1, keepdims=True)
    return ((xf - mu) * lax.rsqrt(var + LN_EPS) * g + b).astype(x.dtype)


def depthwise_conv(u, w, b, pad_lo, pad_hi):
    y = lax.conv_general_dilated(u, w[:, None, :].astype(u.dtype), window_strides=(1,),
                                 padding=[(pad_lo, pad_hi)], dimension_numbers=('NWC', 'WIO', 'NWC'),
                                 feature_group_count=u.shape[-1])
    return y + b.astype(u.dtype)


def pool_mixer(u, w_grp, scale):
    bt, seq_len, _ = u.shape
    ug = u.astype(jnp.float32).reshape(bt, seq_len, N_POOL, POOL_GROUP)
    csum = jnp.concatenate([jnp.zeros_like(ug[:, :1]), jnp.cumsum(ug, axis=1)], axis=1)
    t = np.arange(seq_len)
    outs = []
    for gi, w in enumerate(POOL_WINDOWS):
        start = np.maximum(t - (w - 1) // 2, 0)
        end = np.minimum(t + w // 2, seq_len - 1) + 1
        cg = csum[:, :, gi]
        cnt = jnp.asarray(end - start, jnp.float32)[None, :, None]
        outs.append((cg[:, end] - cg[:, start]) / cnt - ug[:, :, gi])
    pooled = jnp.stack(outs, axis=2).astype(u.dtype)
    y = jnp.einsum('blgc,gce->blge', pooled, w_grp)
    return y.reshape(bt, seq_len, POOL_WIDTH) * scale


def ssd_scan(x, dt, a, bm, cm, h0):
    bt, seq_len, nh, hp = x.shape
    ng, ns = bm.shape[2], bm.shape[3]
    hg = nh // ng
    q = SSD_CHUNK
    nc = seq_len // q
    xc = x.astype(jnp.float32).reshape(bt, nc, q, ng, hg, hp)
    dtc = dt.reshape(bt, nc, q, ng, hg)
    bc = bm.astype(jnp.float32).reshape(bt, nc, q, ng, ns)
    cc = cm.astype(jnp.float32).reshape(bt, nc, q, ng, ns)
    acum = jnp.cumsum(dtc * a.reshape(ng, hg), axis=2)
    causal = np.tril(np.ones((q, q), bool))[:, :, None, None]
    seg = acum[:, :, :, None] - acum[:, :, None, :]
    decay = jnp.exp(jnp.where(causal, seg, -jnp.inf))
    xdt = xc * dtc[..., None]
    cb = jnp.einsum('bcign,bcjgn->bcijg', cc, bc)
    y_diag = jnp.einsum('bcijg,bcijgh,bcjghp->bcighp', cb, decay, xdt)
    decay_end = jnp.exp(acum[:, :, -1:] - acum)
    s_chunk = jnp.einsum('bcjgn,bcjgh,bcjghp->bcghpn', bc, decay_end, xdt)
    chunk_decay = jnp.exp(acum[:, :, -1])

    def step(h, inp):
        s_c, d_c = inp
        return h * d_c[..., None, None] + s_c, h

    h_init = h0.astype(jnp.float32).reshape(bt, ng, hg, hp, ns)
    h_final, h_in = lax.scan(step, h_init, (jnp.moveaxis(s_chunk, 1, 0), jnp.moveaxis(chunk_decay, 1, 0)))
    h_in = jnp.moveaxis(h_in, 0, 1)
    y_off = jnp.einsum('bcign,bcigh,bcghpn->bcighp', cc, jnp.exp(acum), h_in)
    y = (y_diag + y_off).reshape(bt, seq_len, nh, hp)
    return y, h_final.reshape(bt, nh, hp, ns)


def ssd_mixer(z, xbc, dt_raw, conv_w, conv_b, dt_bias, a_log, d_skip, norm_g, h0):
    bt, seq_len, _ = z.shape
    xbc = jax.nn.silu(depthwise_conv(xbc, conv_w, conv_b, (SSD_CONV - 1) // 2, SSD_CONV // 2))
    xs, bm, cm = jnp.split(xbc, [SSD_INNER, SSD_INNER + SSD_GROUPS * SSD_STATE], axis=-1)
    xs = xs.reshape(bt, seq_len, SSD_HEADS, SSD_HEADDIM)
    bm = bm.reshape(bt, seq_len, SSD_GROUPS, SSD_STATE)
    cm = cm.reshape(bt, seq_len, SSD_GROUPS, SSD_STATE)
    dt = jax.nn.softplus(dt_raw.astype(jnp.float32).reshape(bt, seq_len, 2, SSD_HEADS) + dt_bias.astype(jnp.float32))
    a = -jnp.exp(a_log.astype(jnp.float32))
    y_f, h_f = ssd_scan(xs, dt[:, :, 0], a[0], bm, cm, h0[:, 0])
    y_b, h_b = ssd_scan(xs[:, ::-1], dt[:, ::-1, 1], a[1], bm[:, ::-1], cm[:, ::-1], h0[:, 1])
    y = y_f + y_b[:, ::-1] + xs.astype(jnp.float32) * d_skip[:, None]
    y = y.reshape(bt, seq_len, SSD_INNER) * jax.nn.silu(z.astype(jnp.float32))
    yg = y.reshape(bt, seq_len, SSD_GROUPS, SSD_INNER // SSD_GROUPS)
    yg = yg * lax.rsqrt(jnp.mean(jnp.square(yg), axis=-1, keepdims=True) + LN_EPS)
    y = yg.reshape(bt, seq_len, SSD_INNER) * norm_g
    return y.astype(z.dtype), jnp.stack([h_f, h_b], axis=1)


def dense_attention(q, k, v):
    bt, lq, nh, hd = q.shape
    scale = hd ** -0.5
    qb = jnp.moveaxis(q.reshape(bt, lq // ATTN_QBLOCK, ATTN_QBLOCK, nh, hd), 1, 0)

    def block(qi):
        s = jnp.einsum('bqhd,bkhd->bhqk', qi, k).astype(jnp.float32) * scale
        p = jax.nn.softmax(s, axis=-1).astype(v.dtype)
        return jnp.einsum('bhqk,bkhd->bqhd', p, v)

    out = lax.map(block, qb)
    return jnp.moveaxis(out, 0, 1).reshape(bt, lq, nh, hd)


def neighborhood_attention(q, k, v, k_ctx, v_ctx, rpb):
    bt, seq_len, nh, hd = q.shape
    rows = seq_len // GRID_W
    kr = min(NA_ROWS, rows)
    ncb = GRID_W // NA_QCOLS
    scale = hd ** -0.5
    qcol = np.arange(GRID_W).reshape(ncb, NA_QCOLS)
    cstart = np.clip(qcol - NA_COLS // 2, 0, GRID_W - NA_COLS)
    bstart = np.clip(np.arange(ncb) * NA_QCOLS - NA_COLS // 2, 0, GRID_W - NA_KCOLS)
    kcol = bstart[:, None] + np.arange(NA_KCOLS)
    col_valid = (kcol[:, None, :] >= cstart[..., None]) & (kcol[:, None, :] < cstart[..., None] + NA_COLS)
    dc_idx = np.clip(kcol[:, None, :] - qcol[..., None], -(NA_COLS - 1), NA_COLS - 1) + NA_COLS - 1
    kg = k.reshape(bt, rows, GRID_W, nh, hd)
    vg = v.reshape(bt, rows, GRID_W, nh, hd)
    qg = jnp.moveaxis(q.reshape(bt, rows, ncb, NA_QCOLS, nh, hd), 1, 0)
    n_nb = kr * NA_KCOLS

    def row_block(args):
        q_r, r = args
        rs = jnp.clip(r - kr // 2, 0, rows - kr)
        k_r = lax.dynamic_slice_in_dim(kg, rs, kr, axis=1)[:, :, kcol]
        v_r = lax.dynamic_slice_in_dim(vg, rs, kr, axis=1)[:, :, kcol]
        s_nb = jnp.einsum('bjqhd,bkjwhd->bjhqkw', q_r, k_r).astype(jnp.float32) * scale
        dr_idx = rs + jnp.arange(kr) - r + NA_ROWS - 1
        bias = jnp.transpose(rpb[:, dr_idx][:, :, dc_idx], (2, 0, 3, 1, 4))
        s_nb = jnp.where(col_valid[:, None, :, None, :], s_nb + bias, -jnp.inf)
        s_ctx = jnp.einsum('bjqhd,bkhd->bjhqk', q_r, k_ctx).astype(jnp.float32) * scale
        s = jnp.concatenate([s_nb.reshape(bt, ncb, nh, NA_QCOLS, n_nb), s_ctx], axis=-1)
        p = jax.nn.softmax(s, axis=-1).astype(v.dtype)
        p_nb = p[..., :n_nb].reshape(bt, ncb, nh, NA_QCOLS, kr, NA_KCOLS)
        return (jnp.einsum('bjhqkw,bkjwhd->bjqhd', p_nb, v_r)
                + jnp.einsum('bjhqk,bkhd->bjqhd', p[..., n_nb:], v_ctx))

    out = lax.map(row_block, (qg, jnp.arange(rows)))
    return jnp.moveaxis(out, 0, 1).reshape(bt, seq_len, nh, hd)


def conformer_conv(u, dw_w, dw_b, ln_g, ln_b):
    a, gte = jnp.split(u, 2, axis=-1)
    h = a * jax.nn.sigmoid(gte)
    h = depthwise_conv(h, dw_w, dw_b, CONF_KERNEL // 2, CONF_KERNEL // 2)
    return jax.nn.silu(layer_norm(h, ln_g, ln_b))


def token_mixer(h, p, ctx_kv, h0):
    bt, seq_len, _ = h.shape
    u = jnp.einsum('bld,de->ble', h, p['w_in'])
    u_pool, u_z, u_xbc, u_dt, u_qkv, u_conf, u_gate = jnp.split(u, IN_OFFSETS, axis=-1)
    y_pool = pool_mixer(u_pool, p['pool_w'], p['pool_scale'])
    if h0 is None:
        h0 = jnp.zeros((bt, 2, SSD_HEADS, SSD_HEADDIM, SSD_STATE), jnp.float32)
    y_ssd, h_fin = ssd_mixer(u_z, u_xbc, u_dt, p['ssd_conv_w'], p['ssd_conv_b'], p['ssd_dt_bias'],
                             p['ssd_a_log'], p['ssd_d'], p['ssd_norm_g'], h0)
    qkv = u_qkv.reshape(bt, seq_len, 3, NA_HEADS, NA_HEADDIM)
    q, k, v = qkv[:, :, 0], qkv[:, :, 1], qkv[:, :, 2]
    if ctx_kv is None:
        y_na = dense_attention(q, k, v)
    else:
        y_na = neighborhood_attention(q, k, v, ctx_kv[0], ctx_kv[1], p['na_rpb'])
    y_conf = conformer_conv(u_conf, p['conf_dw_w'], p['conf_dw_b'], p['conf_ln_g'], p['conf_ln_b'])
    gates = jax.nn.sigmoid(u_gate.astype(jnp.float32)).astype(h.dtype).reshape(bt, seq_len, N_BRANCH, D_MODEL)
    merged = (gates[:, :, 0] * (y_pool @ p['w_br_pool'])
              + gates[:, :, 1] * (y_ssd @ p['w_br_ssd'])
              + gates[:, :, 2] * (y_na.reshape(bt, seq_len, NA_WIDTH) @ p['w_br_na'])
              + gates[:, :, 3] * (y_conf @ p['w_br_conf']))
    return merged @ p['w_out'], k, v, h_fin


def expert_choice_ffn(h, w_router, w_gate, w_up, w_down):
    bt, tokens, _ = h.shape
    cap = EC_CAPACITY * tokens // N_EXPERTS
    aff = jax.nn.softmax(jnp.einsum('btd,de->bte', h, w_router).astype(jnp.float32), axis=-1)
    g, idx = lax.top_k(jnp.swapaxes(aff, 1, 2), cap)
    bidx = jnp.arange(bt)[:, None, None]
    xe = h[bidx, idx]
    hid = jax.nn.silu(jnp.einsum('becd,edf->becf', xe, w_gate)) * jnp.einsum('becd,edf->becf', xe, w_up)
    ye = jnp.einsum('becf,efd->becd', hid, w_down) * g[..., None].astype(h.dtype)
    return jnp.zeros_like(h).at[bidx, idx].add(ye)


def trunk_layer(x, cond, p, ctx_kv, h0):
    m = jax.nn.silu(cond.reshape(-1, 1, D_MODEL)) @ p['w_mod'] + p['b_mod']
    sh1, sc1, g1, sh2, sc2, g2 = jnp.split(m, 6, axis=-1)
    mix, k, v, h_fin = token_mixer(x * (1 + sc1) + sh1, p, ctx_kv, h0)
    x = layer_norm(DEEPNORM_ALPHA * x + g1 * mix, p['ln1_g'], p['ln1_b'])
    ffn = expert_choice_ffn(x * (1 + sc2) + sh2, p['w_router'], p['w_e_gate'], p['w_e_up'], p['w_e_down'])
    x = layer_norm(DEEPNORM_ALPHA * x + g2 * ffn, p['ln2_g'], p['ln2_b'])
    return x, k, v, h_fin


def _normal(key, shape, scale):
    return jax.random.normal(key, shape, jnp.float32) * scale


def setup_inputs(seed: int = 0) -> dict:
    key = jax.random.key(seed)
    k = jax.random.split(key, 40)
    beta = DEEPNORM_BETA
    dt0 = jnp.exp(jax.random.uniform(k[11], (DEPTH, 2, SSD_HEADS), jnp.float32, math.log(1e-3), math.log(1e-1)))
    return {
        'x_prompt': _normal(k[0], (BATCH, SEQ, D_MODEL), 1.0),
        'x_sample': _normal(k[1], (DEC_BATCH, DEC_SEQ, D_MODEL), 1.0),
        'cache_k': _normal(k[2], (DEC_BATCH, DEPTH, PAST_LEN, NA_HEADS, NA_HEADDIM), 1.0),
        'cache_v': _normal(k[3], (DEC_BATCH, DEPTH, PAST_LEN, NA_HEADS, NA_HEADDIM), 1.0),
        'state_ssm': _normal(k[4], (DEC_BATCH, DEPTH, 2, SSD_HEADS, SSD_HEADDIM, SSD_STATE), 0.5),
        'c': _normal(k[5], (DEC_BATCH, D_MODEL), 1.0),
        'c_ctx': _normal(k[6], (D_MODEL,), 1.0),
        'w_mod': _normal(k[7], (DEPTH, D_MODEL, 6 * D_MODEL), 0.5 * D_MODEL ** -0.5),
        'b_mod': _normal(k[8], (DEPTH, 6 * D_MODEL), 0.02),
        'w_in': _normal(k[9], (DEPTH, D_MODEL, IN_WIDTH), D_MODEL ** -0.5),
        'pool_w': _normal(k[10], (DEPTH, N_POOL, POOL_GROUP, POOL_GROUP), POOL_GROUP ** -0.5),
        'pool_scale': 1.0 + _normal(k[12], (DEPTH, POOL_WIDTH), 0.1),
        'ssd_conv_w': _normal(k[13], (DEPTH, SSD_CONV, SSD_XBC), SSD_CONV ** -0.5),
        'ssd_conv_b': _normal(k[14], (DEPTH, SSD_XBC), 0.02),
        'ssd_dt_bias': dt0 + jnp.log(-jnp.expm1(-dt0)),
        'ssd_a_log': jnp.log(jax.random.uniform(k[15], (DEPTH, 2, SSD_HEADS), jnp.float32, 1.0, 16.0)),
        'ssd_d': 1.0 + _normal(k[16], (DEPTH, SSD_HEADS), 0.1),
        'ssd_norm_g': 1.0 + _normal(k[17], (DEPTH, SSD_INNER), 0.02),
        'na_rpb': _normal(k[18], (DEPTH, NA_HEADS, 2 * NA_ROWS - 1, 2 * NA_COLS - 1), 0.02),
        'conf_dw_w': _normal(k[19], (DEPTH, CONF_KERNEL, CONF_WIDTH), CONF_KERNEL ** -0.5),
        'conf_dw_b': _normal(k[20], (DEPTH, CONF_WIDTH), 0.02),
        'conf_ln_g': 1.0 + _normal(k[21], (DEPTH, CONF_WIDTH), 0.02),
        'conf_ln_b': _normal(k[22], (DEPTH, CONF_WIDTH), 0.02),
        'w_br_pool': _normal(k[23], (DEPTH, POOL_WIDTH, D_MODEL), POOL_WIDTH ** -0.5),
        'w_br_ssd': _normal(k[24], (DEPTH, SSD_INNER, D_MODEL), SSD_INNER ** -0.5),
        'w_br_na': _normal(k[25], (DEPTH, NA_WIDTH, D_MODEL), NA_WIDTH ** -0.5),
        'w_br_conf': _normal(k[26], (DEPTH, CONF_WIDTH, D_MODEL), CONF_WIDTH ** -0.5),
        'w_out': _normal(k[27], (DEPTH, D_MODEL, D_MODEL), beta * D_MODEL ** -0.5),
        'ln1_g': 1.0 + _normal(k[28], (DEPTH, D_MODEL), 0.02),
        'ln1_b': _normal(k[29], (DEPTH, D_MODEL), 0.02),
        'w_router': _normal(k[30], (DEPTH, D_MODEL, N_EXPERTS), D_MODEL ** -0.5),
        'w_e_gate': _normal(k[31], (DEPTH, N_EXPERTS, D_MODEL, EXPERT_FF), D_MODEL ** -0.5),
        'w_e_up': _normal(k[32], (DEPTH, N_EXPERTS, D_MODEL, EXPERT_FF), D_MODEL ** -0.5),
        'w_e_down': _normal(k[33], (DEPTH, N_EXPERTS, EXPERT_FF, D_MODEL), beta * EXPERT_FF ** -0.5),
        'ln2_g': 1.0 + _normal(k[34], (DEPTH, D_MODEL), 0.02),
        'ln2_b': _normal(k[35], (DEPTH, D_MODEL), 0.02),
    }


def reference(x_prompt, x_sample, cache_k, cache_v, state_ssm, c, c_ctx, w_mod, b_mod, w_in, pool_w,
              pool_scale, ssd_conv_w, ssd_conv_b, ssd_dt_bias, ssd_a_log, ssd_d, ssd_norm_g, na_rpb,
              conf_dw_w, conf_dw_b, conf_ln_g, conf_ln_b, w_br_pool, w_br_ssd, w_br_na, w_br_conf, w_out,
              ln1_g, ln1_b, w_router, w_e_gate, w_e_up, w_e_down, ln2_g, ln2_b):
    stacked = {
        'w_mod': w_mod, 'b_mod': b_mod, 'w_in': w_in, 'pool_w': pool_w, 'pool_scale': pool_scale,
        'ssd_conv_w': ssd_conv_w, 'ssd_conv_b': ssd_conv_b, 'ssd_dt_bias': ssd_dt_bias,
        'ssd_a_log': ssd_a_log, 'ssd_d': ssd_d, 'ssd_norm_g': ssd_norm_g, 'na_rpb': na_rpb,
        'conf_dw_w': conf_dw_w, 'conf_dw_b': conf_dw_b, 'conf_ln_g': conf_ln_g, 'conf_ln_b': conf_ln_b,
        'w_br_pool': w_br_pool, 'w_br_ssd': w_br_ssd, 'w_br_na': w_br_na, 'w_br_conf': w_br_conf,
        'w_out': w_out, 'ln1_g': ln1_g, 'ln1_b': ln1_b, 'w_router': w_router, 'w_e_gate': w_e_gate,
        'w_e_up': w_e_up, 'w_e_down': w_e_down, 'ln2_g': ln2_g, 'ln2_b': ln2_b,
    }
    xp, xs = x_prompt, x_sample
    ks, vs, hs = [], [], []
    for i in range(DEPTH):
        p = {name: arr[i] for name, arr in stacked.items()}
        xp, k_i, v_i, h_i = trunk_layer(xp, c_ctx, p, None, None)
        ks.append(k_i)
        vs.append(v_i)
        hs.append(h_i.astype(xp.dtype))
        xs, _, _, _ = trunk_layer(xs, c, p, (cache_k[:, i], cache_v[:, i]), state_ssm[:, i])
    new_cache_k = jnp.stack(ks, axis=1)
    new_cache_v = jnp.stack(vs, axis=1)
    new_state_ssm = jnp.stack(hs, axis=1)
    return (xp, xs, new_cache_k, new_cache_v, new_state_ssm)
```

```python
import functools
import math

import numpy as np
import jax
import jax.numpy as jnp
from jax import lax
from jax.experimental import pallas as pl
from jax.experimental.pallas import tpu as pltpu

F32 = jnp.float32
BF16 = jnp.bfloat16

D_MODEL = 4096
BATCH = 32
SEQ = 256
DEPTH = 4
DEC_BATCH = 2
DEC_SEQ = 2048
PAST_LEN = 256
GRID_W = 64
POOL_WIDTH = 1024
POOL_WINDOWS = (2, 4, 8, 16)
N_POOL = len(POOL_WINDOWS)
POOL_GROUP = POOL_WIDTH // N_POOL
SSD_INNER = 2048
SSD_HEADDIM = 64
SSD_HEADS = SSD_INNER // SSD_HEADDIM
SSD_GROUPS = 4
SSD_STATE = 128
SSD_CONV = 4
SSD_CHUNK = 128
SSD_XBC = SSD_INNER + 2 * SSD_GROUPS * SSD_STATE
NA_HEADS = 16
NA_HEADDIM = 64
NA_WIDTH = NA_HEADS * NA_HEADDIM
NA_ROWS = 8
NA_COLS = 16
NA_QCOLS = 16
NA_KCOLS = NA_QCOLS + NA_COLS
ATTN_QBLOCK = 128
CONF_WIDTH = 1024
CONF_KERNEL = 31
N_BRANCH = 4
IN_SIZES = (POOL_WIDTH, SSD_INNER, SSD_XBC, 2 * SSD_HEADS, 3 * NA_WIDTH, 2 * CONF_WIDTH, N_BRANCH * D_MODEL)
IN_WIDTH = sum(IN_SIZES)
IN_STARTS = tuple(sum(IN_SIZES[:i]) for i in range(len(IN_SIZES)))
N_EXPERTS = 16
EXPERT_FF = 1024
EC_CAPACITY = 2
DEEPNORM_ALPHA = (2 * DEPTH) ** 0.25
LN_EPS = 1e-5

N_CTX_ROWS = BATCH * SEQ
N_LAT_ROWS = DEC_BATCH * DEC_SEQ
M_TOK = N_CTX_ROWS + N_LAT_ROWS
N_SEG = 1 + DEC_BATCH
MOD_ROWS = 8

LANE = 128
VMEM_LIMIT_BYTES = 56 * 1024 * 1024


def _cparams(n_grid):
    return pltpu.CompilerParams(dimension_semantics=("arbitrary",) * n_grid,
                                vmem_limit_bytes=VMEM_LIMIT_BYTES)


def _mm_kernel(*refs, n_w, shift, prologue, epilogue, has_bias, has_scale):
    it = iter(refs)
    a_ref = next(it)
    w_refs = []
    for _ in range(n_w):
        wm = next(it)
        wn = next(it) if shift else None
        w_refs.append((wm, wn))
    bias_ref = next(it) if has_bias else None
    scale_ref = next(it) if has_scale else None
    o_ref = next(it)
    wbf = [next(it) for _ in range(n_w)]

    @pl.when(pl.program_id(2) == 0)
    def _():
        for (wm, wn), dst in zip(w_refs, wbf):
            if wn is None:
                dst[...] = wm[...].astype(BF16)
            else:
                dst[...] = jnp.concatenate([wm[:, shift:], wn[:, :shift]], axis=1).astype(BF16)

    a = a_ref[...]
    if prologue == "silu":
        a = a.astype(F32)
        a = (a * jax.nn.sigmoid(a)).astype(BF16)
    acc = [jnp.dot(a, w[...], preferred_element_type=F32) for w in wbf]
    if epilogue == "swiglu":
        g, u = acc
        r = (g * jax.nn.sigmoid(g)) * u
    else:
        r = acc[0]
        if has_bias:
            r = r + bias_ref[...]
        if epilogue == "sigmoid":
            r = jax.nn.sigmoid(r)
        if has_scale:
            r = r * scale_ref[...]
    o_ref[...] = r.astype(o_ref.dtype)


def _matmul(a, ws, *, lead, batched_w, col_start, n_cols, tm, tn, out_dtype,
            prologue=None, epilogue=None, bias=None, bias_lead=(), scale=None, name="mm"):
    G, M, K = a.shape
    shift = col_start % LANE
    base = col_start - shift
    assert base % tn == 0 and n_cols % tn == 0 and M % tm == 0
    assert shift == 0 or tn % LANE == 0
    n_j = n_cols // tn
    nl = len(lead)
    sq = (None,) * nl

    def w_main_map(g, j, i):
        return lead + ((g,) if batched_w else ()) + (0, base // tn + j)

    def w_next_map(g, j, i):
        return lead + ((g,) if batched_w else ()) + (0, (base + (j + 1) * tn) // LANE)

    gsq = (None,) if batched_w else ()
    in_specs = [pl.BlockSpec((None, tm, K), lambda g, j, i: (g, i, 0))]
    args = [a]
    for w in ws:
        in_specs.append(pl.BlockSpec(sq + gsq + (K, tn), w_main_map))
        args.append(w)
        if shift:
            in_specs.append(pl.BlockSpec(sq + gsq + (K, LANE), w_next_map))
            args.append(w)
    if bias is not None:
        nbl = len(bias_lead)
        in_specs.append(pl.BlockSpec((None,) * nbl + (1, tn),
                                     lambda g, j, i: bias_lead + (0, col_start // tn + j)))
        args.append(bias)
    if scale is not None:
        in_specs.append(pl.BlockSpec((None, tm, 1), lambda g, j, i: (g, i, 0)))
        args.append(scale)
    kern = functools.partial(_mm_kernel, n_w=len(ws), shift=shift, prologue=prologue,
                             epilogue=epilogue, has_bias=bias is not None,
                             has_scale=scale is not None)
    return pl.pallas_call(
        kern,
        grid=(G, n_j, M // tm),
        in_specs=in_specs,
        out_specs=pl.BlockSpec((None, tm, tn), lambda g, j, i: (g, i, j)),
        out_shape=jax.ShapeDtypeStruct((G, M, n_cols), out_dtype),
        scratch_shapes=[pltpu.VMEM((K, tn), BF16) for _ in ws],
        compiler_params=_cparams(3),
        name=name,
    )(*args)


def _merge_kernel(y0, y1, y2, y3, w0, w1, w2, w3, g0, g1, g2, g3, o_ref, b0, b1, b2, b3):
    ys = (y0, y1, y2, y3)
    wsrc = (w0, w1, w2, w3)
    gs = (g0, g1, g2, g3)
    wbf = (b0, b1, b2, b3)

    @pl.when(pl.program_id(1) == 0)
    def _():
        for s, d in zip(wsrc, wbf):
            d[...] = s[...].astype(BF16)

    acc = None
    for k in range(N_BRANCH):
        p = jnp.dot(ys[k][...], wbf[k][...], preferred_element_type=F32) * gs[k][...].astype(F32)
        acc = p if acc is None else acc + p
    o_ref[...] = acc.astype(o_ref.dtype)


def _merge(ys, ws, layer, gates, *, tm=512, tn=512):
    M = ys[0].shape[0]
    nj = D_MODEL // tn
    in_specs = [pl.BlockSpec((tm, y.shape[1]), lambda j, i: (i, 0)) for y in ys]
    in_specs += [pl.BlockSpec((None, w.shape[1], tn), lambda j, i: (layer, 0, j)) for w in ws]
    in_specs += [pl.BlockSpec((tm, tn), functools.partial(lambda j, i, k: (i, k * nj + j), k=k))
                 for k in range(N_BRANCH)]
    return pl.pallas_call(
        _merge_kernel,
        grid=(nj, M // tm),
        in_specs=in_specs,
        out_specs=pl.BlockSpec((tm, tn), lambda j, i: (i, j)),
        out_shape=jax.ShapeDtypeStruct((M, D_MODEL), BF16),
        scratch_shapes=[pltpu.VMEM((w.shape[1], tn), BF16) for w in ws],
        compiler_params=_cparams(2),
        name="branch_merge",
    )(*ys, *ws, *([gates] * N_BRANCH))


ROW_TILE = 256


def _segment_of_tile(i):
    r0 = i * ROW_TILE
    return jnp.where(r0 < N_CTX_ROWS, 0, 1 + (r0 - N_CTX_ROWS) // DEC_SEQ)


def _modulate_kernel(x_ref, sh_ref, sc_ref, h_ref):
    seg = _segment_of_tile(pl.program_id(0))
    sh = sh_ref[pl.ds(seg, 1), :]
    sc = sc_ref[pl.ds(seg, 1), :]
    h_ref[...] = (x_ref[...] * (1.0 + sc) + sh).astype(h_ref.dtype)


def _mod_spec(layer, k):
    return pl.BlockSpec((None, MOD_ROWS, D_MODEL), lambda i: (layer, 0, k))


def _modulate(x, mods, layer):
    return pl.pallas_call(
        _modulate_kernel,
        grid=(M_TOK // ROW_TILE,),
        in_specs=[pl.BlockSpec((ROW_TILE, D_MODEL), lambda i: (i, 0)),
                  _mod_spec(layer, 0), _mod_spec(layer, 1)],
        out_specs=pl.BlockSpec((ROW_TILE, D_MODEL), lambda i: (i, 0)),
        out_shape=jax.ShapeDtypeStruct((M_TOK, D_MODEL), BF16),
        compiler_params=_cparams(1),
        name="modulate",
    )(x, mods, mods)


def _deepnorm_kernel(x_ref, y_ref, g_ref, lng_ref, lnb_ref, sh_ref, sc_ref, xo_ref, h_ref):
    seg = _segment_of_tile(pl.program_id(0))
    g = g_ref[pl.ds(seg, 1), :]
    v = DEEPNORM_ALPHA * x_ref[...] + g * y_ref[...].astype(F32)
    mu = jnp.mean(v, axis=-1, keepdims=True)
    d = v - mu
    var = jnp.mean(d * d, axis=-1, keepdims=True)
    xn = d * lax.rsqrt(var + LN_EPS) * lng_ref[...] + lnb_ref[...]
    xo_ref[...] = xn
    sh = sh_ref[pl.ds(seg, 1), :]
    sc = sc_ref[pl.ds(seg, 1), :]
    h_ref[...] = (xn * (1.0 + sc) + sh).astype(h_ref.dtype)


def _deepnorm(x, y, mods, layer, gate_chunk, ln_g, ln_b, next_layer, next_chunk, h_dtype):
    row = pl.BlockSpec((ROW_TILE, D_MODEL), lambda i: (i, 0))
    vec = pl.BlockSpec((None, 1, D_MODEL), lambda i: (layer, 0, 0))
    return pl.pallas_call(
        _deepnorm_kernel,
        grid=(M_TOK // ROW_TILE,),
        in_specs=[row, row, _mod_spec(layer, gate_chunk), vec, vec,
                  _mod_spec(next_layer, next_chunk), _mod_spec(next_layer, next_chunk + 1)],
        out_specs=[row, row],
        out_shape=[jax.ShapeDtypeStruct((M_TOK, D_MODEL), F32),
                   jax.ShapeDtypeStruct((M_TOK, D_MODEL), h_dtype)],
        compiler_params=_cparams(1),
        name="deepnorm",
    )(x, y, mods, ln_g.reshape(DEPTH, 1, D_MODEL), ln_b.reshape(DEPTH, 1, D_MODEL), mods, mods)


def _layer_norm(x, g, b):
    mu = jnp.mean(x, axis=-1, keepdims=True)
    var = jnp.mean(jnp.square(x - mu), axis=-1, keepdims=True)
    return (x - mu) * lax.rsqrt(var + LN_EPS) * g + b


def _depthwise_conv(u, w, b, pad_lo, pad_hi):
    y = lax.conv_general_dilated(u, w[:, None, :], window_strides=(1,), padding=[(pad_lo, pad_hi)],
                                 dimension_numbers=('NWC', 'WIO', 'NWC'), feature_group_count=u.shape[-1])
    return y + b


def _pool_mixer(u, w_grp, scale):
    bt, seq_len, _ = u.shape
    ug = u.reshape(bt, seq_len, N_POOL, POOL_GROUP)
    csum = jnp.concatenate([jnp.zeros_like(ug[:, :1]), jnp.cumsum(ug, axis=1)], axis=1)
    t = np.arange(seq_len)
    outs = []
    for gi, w in enumerate(POOL_WINDOWS):
        start = np.maximum(t - (w - 1) // 2, 0)
        end = np.minimum(t + w // 2, seq_len - 1) + 1
        cg = csum[:, :, gi]
        cnt = jnp.asarray(end - start, jnp.float32)[None, :, None]
        outs.append((cg[:, end] - cg[:, start]) / cnt - ug[:, :, gi])
    pooled = jnp.stack(outs, axis=2)
    y = jnp.einsum('blgc,gce->blge', pooled, w_grp)
    return y.reshape(bt, seq_len, POOL_WIDTH) * scale


def _ssd_scan(x, dt, a, bm, cm, h0):
    bt, seq_len, nh, hp = x.shape
    ng, ns = bm.shape[2], bm.shape[3]
    hg = nh // ng
    q = SSD_CHUNK
    nc = seq_len // q
    xc = x.reshape(bt, nc, q, ng, hg, hp)
    dtc = dt.reshape(bt, nc, q, ng, hg)
    bc = bm.reshape(bt, nc, q, ng, ns)
    cc = cm.reshape(bt, nc, q, ng, ns)
    acum = jnp.cumsum(dtc * a.reshape(ng, hg), axis=2)
    causal = np.tril(np.ones((q, q), bool))[:, :, None, None]
    seg = acum[:, :, :, None] - acum[:, :, None, :]
    decay = jnp.exp(jnp.where(causal, seg, -jnp.inf))
    xdt = xc * dtc[..., None]
    cb = jnp.einsum('bcign,bcjgn->bcijg', cc, bc)
    y_diag = jnp.einsum('bcijg,bcijgh,bcjghp->bcighp', cb, decay, xdt)
    decay_end = jnp.exp(acum[:, :, -1:] - acum)
    s_chunk = jnp.einsum('bcjgn,bcjgh,bcjghp->bcghpn', bc, decay_end, xdt)
    chunk_decay = jnp.exp(acum[:, :, -1])

    def step(h, inp):
        s_c, d_c = inp
        return h * d_c[..., None, None] + s_c, h

    h_init = h0.reshape(bt, ng, hg, hp, ns)
    h_final, h_in = lax.scan(step, h_init, (jnp.moveaxis(s_chunk, 1, 0), jnp.moveaxis(chunk_decay, 1, 0)))
    h_in = jnp.moveaxis(h_in, 0, 1)
    y_off = jnp.einsum('bcign,bcigh,bcghpn->bcighp', cc, jnp.exp(acum), h_in)
    y = (y_diag + y_off).reshape(bt, seq_len, nh, hp)
    return y, h_final.reshape(bt, nh, hp, ns)


def _ssd_mixer(z, xbc, dt_raw, conv_w, conv_b, dt_bias, a_log, d_skip, norm_g, h0):
    bt, seq_len, _ = z.shape
    xbc = jax.nn.silu(_depthwise_conv(xbc, conv_w, conv_b, (SSD_CONV - 1) // 2, SSD_CONV // 2))
    xs, bm, cm = jnp.split(xbc, [SSD_INNER, SSD_INNER + SSD_GROUPS * SSD_STATE], axis=-1)
    xs = xs.reshape(bt, seq_len, SSD_HEADS, SSD_HEADDIM)
    bm = bm.reshape(bt, seq_len, SSD_GROUPS, SSD_STATE)
    cm = cm.reshape(bt, seq_len, SSD_GROUPS, SSD_STATE)
    dt = jax.nn.softplus(dt_raw.reshape(bt, seq_len, 2, SSD_HEADS) + dt_bias)
    a = -jnp.exp(a_log)
    y_f, h_f = _ssd_scan(xs, dt[:, :, 0], a[0], bm, cm, h0[:, 0])
    y_b, h_b = _ssd_scan(xs[:, ::-1], dt[:, ::-1, 1], a[1], bm[:, ::-1], cm[:, ::-1], h0[:, 1])
    y = y_f + y_b[:, ::-1] + xs * d_skip[:, None]
    y = y.reshape(bt, seq_len, SSD_INNER) * jax.nn.silu(z)
    yg = y.reshape(bt, seq_len, SSD_GROUPS, SSD_INNER // SSD_GROUPS)
    yg = yg * lax.rsqrt(jnp.mean(jnp.square(yg), axis=-1, keepdims=True) + LN_EPS)
    y = yg.reshape(bt, seq_len, SSD_INNER) * norm_g
    return y, jnp.stack([h_f, h_b], axis=1)


def _dense_attention(q, k, v):
    bt, lq, nh, hd = q.shape
    scale = hd ** -0.5
    s = jnp.einsum('bqhd,bkhd->bhqk', q, k) * scale
    p = jax.nn.softmax(s, axis=-1)
    return jnp.einsum('bhqk,bkhd->bqhd', p, v)


def _neighborhood_attention(q, k, v, k_ctx, v_ctx, rpb):
    bt, seq_len, nh, hd = q.shape
    rows = seq_len // GRID_W
    kr = min(NA_ROWS, rows)
    ncb = GRID_W // NA_QCOLS
    scale = hd ** -0.5
    qcol = np.arange(GRID_W).reshape(ncb, NA_QCOLS)
    cstart = np.clip(qcol - NA_COLS // 2, 0, GRID_W - NA_COLS)
    bstart = np.clip(np.arange(ncb) * NA_QCOLS - NA_COLS // 2, 0, GRID_W - NA_KCOLS)
    kcol = bstart[:, None] + np.arange(NA_KCOLS)
    col_valid = (kcol[:, None, :] >= cstart[..., None]) & (kcol[:, None, :] < cstart[..., None] + NA_COLS)
    dc_idx = np.clip(kcol[:, None, :] - qcol[..., None], -(NA_COLS - 1), NA_COLS - 1) + NA_COLS - 1
    kg = k.reshape(bt, rows, GRID_W, nh, hd)
    vg = v.reshape(bt, rows, GRID_W, nh, hd)
    qg = jnp.moveaxis(q.reshape(bt, rows, ncb, NA_QCOLS, nh, hd), 1, 0)
    n_nb = kr * NA_KCOLS

    def row_block(args):
        q_r, r = args
        rs = jnp.clip(r - kr // 2, 0, rows - kr)
        k_r = lax.dynamic_slice_in_dim(kg, rs, kr, axis=1)[:, :, kcol]
        v_r = lax.dynamic_slice_in_dim(vg, rs, kr, axis=1)[:, :, kcol]
        s_nb = jnp.einsum('bjqhd,bkjwhd->bjhqkw', q_r, k_r) * scale
        dr_idx = rs + jnp.arange(kr) - r + NA_ROWS - 1
        bias = jnp.transpose(rpb[:, dr_idx][:, :, dc_idx], (2, 0, 3, 1, 4))
        s_nb = jnp.where(col_valid[:, None, :, None, :], s_nb + bias, -jnp.inf)
        s_ctx = jnp.einsum('bjqhd,bkhd->bjhqk', q_r, k_ctx) * scale
        s = jnp.concatenate([s_nb.reshape(bt, ncb, nh, NA_QCOLS, n_nb), s_ctx], axis=-1)
        p = jax.nn.softmax(s, axis=-1)
        p_nb = p[..., :n_nb].reshape(bt, ncb, nh, NA_QCOLS, kr, NA_KCOLS)
        return (jnp.einsum('bjhqkw,bkjwhd->bjqhd', p_nb, v_r)
                + jnp.einsum('bjhqk,bkhd->bjqhd', p[..., n_nb:], v_ctx))

    out = lax.map(row_block, (qg, jnp.arange(rows)))
    return jnp.moveaxis(out, 0, 1).reshape(bt, seq_len, nh, hd)


def _conformer_conv(u, dw_w, dw_b, ln_g, ln_b):
    a, gte = jnp.split(u, 2, axis=-1)
    h = a * jax.nn.sigmoid(gte)
    h = _depthwise_conv(h, dw_w, dw_b, CONF_KERNEL // 2, CONF_KERNEL // 2)
    return jax.nn.silu(_layer_norm(h, ln_g, ln_b))


def _split_groups(a):
    c = a.shape[-1]
    return a[:N_CTX_ROWS].reshape(BATCH, SEQ, c), a[N_CTX_ROWS:].reshape(DEC_BATCH, DEC_SEQ, c)


def _join_groups(a_ctx, a_lat):
    c = a_ctx.shape[-1]
    return jnp.concatenate([a_ctx.reshape(N_CTX_ROWS, c), a_lat.reshape(N_LAT_ROWS, c)], axis=0)


BIG_TM = 1024
BIG_TN = 512


def kernel(x_prompt, x_sample, cache_k, cache_v, state_ssm, c, c_ctx, w_mod, b_mod, w_in, pool_w, pool_scale, ssd_conv_w, ssd_conv_b, ssd_dt_bias, ssd_a_log, ssd_d, ssd_norm_g, na_rpb, conf_dw_w, conf_dw_b, conf_ln_g, conf_ln_b, w_br_pool, w_br_ssd, w_br_na, w_br_conf, w_out, ln1_g, ln1_b, w_router, w_e_gate, w_e_up, w_e_down, ln2_g, ln2_b):
    x = jnp.concatenate([x_prompt.reshape(N_CTX_ROWS, D_MODEL), x_sample.reshape(N_LAT_ROWS, D_MODEL)], axis=0)

    cond = jnp.concatenate([c_ctx[None], c, jnp.zeros((MOD_ROWS - N_SEG, D_MODEL), F32)], axis=0)
    mods = jnp.concatenate([
        _matmul(cond[None], [w_mod], lead=(l,), batched_w=False, col_start=0, n_cols=6 * D_MODEL,
                tm=MOD_ROWS, tn=1024, out_dtype=F32, prologue="silu",
                bias=b_mod.reshape(DEPTH, 1, 6 * D_MODEL), bias_lead=(l,), name="mod")
        for l in range(DEPTH)], axis=0)

    h = _modulate(x, mods, 0)
    ks, vs, hs = [], [], []
    for l in range(DEPTH):
        def in_proj(sec, out_dtype=F32, epilogue=None, tn=BIG_TN, width=None):
            return _matmul(h[None], [w_in], lead=(l,), batched_w=False, col_start=IN_STARTS[sec],
                           n_cols=width or IN_SIZES[sec], tm=BIG_TM, tn=tn, out_dtype=out_dtype,
                           epilogue=epilogue, name=f"in_proj{sec}")[0]

        u_pool = in_proj(0)
        u_z = in_proj(1)
        u_xbc = in_proj(2)
        u_dt = in_proj(3, tn=LANE, width=LANE)[:, :2 * SSD_HEADS]
        u_qkv = in_proj(4)
        u_conf = in_proj(5)
        gates = in_proj(6, epilogue="sigmoid")

        y_pool = _join_groups(*[_pool_mixer(t, pool_w[l], pool_scale[l]) for t in _split_groups(u_pool)])
        z_c, z_l = _split_groups(u_z)
        xbc_c, xbc_l = _split_groups(u_xbc)
        dt_c, dt_l = _split_groups(u_dt)
        ssd_args = (ssd_conv_w[l], ssd_conv_b[l], ssd_dt_bias[l], ssd_a_log[l], ssd_d[l], ssd_norm_g[l])
        y_ssd_c, h_fin = _ssd_mixer(z_c, xbc_c, dt_c, *ssd_args,
                                    jnp.zeros((BATCH, 2, SSD_HEADS, SSD_HEADDIM, SSD_STATE), F32))
        y_ssd_l, _ = _ssd_mixer(z_l, xbc_l, dt_l, *ssd_args, state_ssm[:, l])
        y_ssd = _join_groups(y_ssd_c, y_ssd_l)
        qkv_c, qkv_l = _split_groups(u_qkv)
        qkv_c = qkv_c.reshape(BATCH, SEQ, 3, NA_HEADS, NA_HEADDIM)
        qkv_l = qkv_l.reshape(DEC_BATCH, DEC_SEQ, 3, NA_HEADS, NA_HEADDIM)
        y_na_c = _dense_attention(qkv_c[:, :, 0], qkv_c[:, :, 1], qkv_c[:, :, 2])
        y_na_l = _neighborhood_attention(qkv_l[:, :, 0], qkv_l[:, :, 1], qkv_l[:, :, 2],
                                         cache_k[:, l], cache_v[:, l], na_rpb[l])
        y_na = _join_groups(y_na_c.reshape(BATCH, SEQ, NA_WIDTH), y_na_l.reshape(DEC_BATCH, DEC_SEQ, NA_WIDTH))
        ks.append(qkv_c[:, :, 1])
        vs.append(qkv_c[:, :, 2])
        hs.append(h_fin)
        y_conf = _join_groups(*[_conformer_conv(t, conf_dw_w[l], conf_dw_b[l], conf_ln_g[l], conf_ln_b[l])
                                for t in _split_groups(u_conf)])

        merged = _merge([y_pool.astype(BF16), y_ssd.astype(BF16), y_na.astype(BF16), y_conf.astype(BF16)],
                        [w_br_pool, w_br_ssd, w_br_na, w_br_conf], l, gates)
        mix = _matmul(merged[None], [w_out], lead=(l,), batched_w=False, col_start=0, n_cols=D_MODEL,
                      tm=BIG_TM, tn=BIG_TN, out_dtype=F32, name="out_proj")[0]
        x, h2 = _deepnorm(x, mix, mods, l, 2, ln1_g, ln1_b, l, 3, F32)

        def route(hh):
            bt, tokens, _ = hh.shape
            cap = EC_CAPACITY * tokens // N_EXPERTS
            aff = jax.nn.softmax(jnp.einsum('btd,de->bte', hh, w_router[l]), axis=-1)
            g, idx = lax.top_k(jnp.swapaxes(aff, 1, 2), cap)
            bidx = jnp.arange(bt)[:, None, None]
            return g, idx, bidx, hh[bidx, idx]

        h2_c, h2_l = _split_groups(h2)
        g_c, idx_c, bidx_c, xe_c = route(h2_c)
        g_l, idx_l, bidx_l, xe_l = route(h2_l)
        cap_c, cap_l = xe_c.shape[2], xe_l.shape[2]
        xe = jnp.concatenate([jnp.swapaxes(xe_c, 0, 1).reshape(N_EXPERTS, BATCH * cap_c, D_MODEL),
                              jnp.swapaxes(xe_l, 0, 1).reshape(N_EXPERTS, DEC_BATCH * cap_l, D_MODEL)], axis=1)
        ge = jnp.concatenate([jnp.swapaxes(g_c, 0, 1).reshape(N_EXPERTS, BATCH * cap_c, 1),
                              jnp.swapaxes(g_l, 0, 1).reshape(N_EXPERTS, DEC_BATCH * cap_l, 1)], axis=1)
        rows_e = xe.shape[1]
        hid = _matmul(xe.astype(BF16), [w_e_gate, w_e_up], lead=(l,), batched_w=True, col_start=0,
                      n_cols=EXPERT_FF, tm=rows_e, tn=256, out_dtype=BF16, epilogue="swiglu", name="expert_up")
        ye = _matmul(hid, [w_e_down], lead=(l,), batched_w=True, col_start=0, n_cols=D_MODEL,
                     tm=rows_e, tn=1024, out_dtype=F32, scale=ge, name="expert_down")
        ye_c = jnp.swapaxes(ye[:, :BATCH * cap_c].reshape(N_EXPERTS, BATCH, cap_c, D_MODEL), 0, 1)
        ye_l = jnp.swapaxes(ye[:, BATCH * cap_c:].reshape(N_EXPERTS, DEC_BATCH, cap_l, D_MODEL), 0, 1)
        ffn = _join_groups(jnp.zeros_like(h2_c).at[bidx_c, idx_c].add(ye_c),
                           jnp.zeros_like(h2_l).at[bidx_l, idx_l].add(ye_l))
        nl = min(l + 1, DEPTH - 1)
        x, h = _deepnorm(x, ffn, mods, l, 5, ln2_g, ln2_b, nl, 0, BF16)

    new_cache_k = jnp.stack(ks, axis=1)
    new_cache_v = jnp.stack(vs, axis=1)
    new_state_ssm = jnp.stack(hs, axis=1)
    return (x[:N_CTX_ROWS].reshape(BATCH, SEQ, D_MODEL), x[N_CTX_ROWS:].reshape(DEC_BATCH, DEC_SEQ, D_MODEL),
            new_cache_k, new_cache_v, new_state_ssm)
```

```python
import collections
import functools

import jax
import jax.numpy as jnp
from jax import lax
from jax.experimental import pallas as pl
from jax.experimental.pallas import tpu as pltpu

F32 = jnp.float32
BF16 = jnp.bfloat16
HIGHEST = lax.Precision.HIGHEST

D_MODEL = 4096
BATCH = 32
SEQ = 256
DEPTH = 4
DEC_BATCH = 2
DEC_SEQ = 2048
PAST_LEN = 256
GRID_W = 64
POOL_WIDTH = 1024
POOL_WINDOWS = (2, 4, 8, 16)
N_POOL = len(POOL_WINDOWS)
POOL_GROUP = POOL_WIDTH // N_POOL
SSD_INNER = 2048
SSD_HEADDIM = 64
SSD_HEADS = SSD_INNER // SSD_HEADDIM
SSD_GROUPS = 4
SSD_GROUP_WIDTH = SSD_INNER // SSD_GROUPS
SSD_GROUP_HEADS = SSD_HEADS // SSD_GROUPS
SSD_STATE = 128
SSD_CONV = 4
SSD_CHUNK = 128
SSD_XBC = SSD_INNER + 2 * SSD_GROUPS * SSD_STATE
NA_HEADS = 16
NA_HEADDIM = 64
NA_WIDTH = NA_HEADS * NA_HEADDIM
NA_ROWS = 8
NA_COLS = 16
CONF_WIDTH = 1024
CONF_KERNEL = 31
N_BRANCH = 4
IN_SIZES = (POOL_WIDTH, SSD_INNER, SSD_XBC, 2 * SSD_HEADS, 3 * NA_WIDTH, 2 * CONF_WIDTH, N_BRANCH * D_MODEL)
IN_STARTS = tuple(sum(IN_SIZES[:i]) for i in range(len(IN_SIZES)))
N_EXPERTS = 16
EXPERT_FF = 1024
EC_CAPACITY = 2
DEEPNORM_ALPHA = (2 * DEPTH) ** 0.25
LN_EPS = 1e-5
MASKED = -1e30

MOD_ROWS = 8
LANE = 128
SUBLANE = 8
VMEM_LIMIT_BYTES = 56 * 1024 * 1024
ROW_TILE = 256

Layout = collections.namedtuple("Layout", "n_ctx ctx_len n_lat lat_len")
FULL = Layout(BATCH, SEQ, DEC_BATCH, DEC_SEQ)


def _n_ctx_rows(lay):
    return lay.n_ctx * lay.ctx_len


def _n_rows(lay):
    return lay.n_ctx * lay.ctx_len + lay.n_lat * lay.lat_len


def _cparams(n_grid):
    return pltpu.CompilerParams(dimension_semantics=("arbitrary",) * n_grid,
                                vmem_limit_bytes=VMEM_LIMIT_BYTES)


def _silu(x):
    return x * jax.nn.sigmoid(x)


def _mm_kernel(*refs, n_w, shift, prologue, epilogue, has_bias, has_scale):
    it = iter(refs)
    a_ref = next(it)
    w_refs = []
    for _ in range(n_w):
        wm = next(it)
        wn = next(it) if shift else None
        w_refs.append((wm, wn))
    bias_ref = next(it) if has_bias else None
    scale_ref = next(it) if has_scale else None
    o_ref = next(it)
    wbf = [next(it) for _ in range(n_w)]

    @pl.when(pl.program_id(2) == 0)
    def _():
        for (wm, wn), dst in zip(w_refs, wbf):
            if wn is None:
                dst[...] = wm[...].astype(BF16)
            else:
                dst[...] = jnp.concatenate([wm[:, shift:], wn[:, :shift]], axis=1).astype(BF16)

    a = a_ref[...]
    if prologue == "silu":
        a = _silu(a.astype(F32)).astype(BF16)
    acc = [jnp.dot(a, w[...], preferred_element_type=F32) for w in wbf]
    if epilogue == "swiglu":
        g, u = acc
        r = _silu(g) * u
    else:
        r = acc[0]
        if has_bias:
            r = r + bias_ref[...]
        if epilogue == "sigmoid":
            r = jax.nn.sigmoid(r)
        if has_scale:
            r = r * scale_ref[...]
    o_ref[...] = r.astype(o_ref.dtype)


def _matmul(a, ws, *, lead, batched_w, col_start, n_cols, tm, tn, out_dtype,
            prologue=None, epilogue=None, bias=None, bias_lead=(), scale=None, name="mm"):
    G, M, K = a.shape
    shift = col_start % LANE
    base = col_start - shift
    assert base % tn == 0 and n_cols % tn == 0 and M % tm == 0
    assert shift == 0 or tn % LANE == 0
    n_j = n_cols // tn
    sq = (None,) * len(lead)
    gsq = (None,) if batched_w else ()

    def w_main_map(g, j, i):
        return lead + ((g,) if batched_w else ()) + (0, base // tn + j)

    def w_next_map(g, j, i):
        return lead + ((g,) if batched_w else ()) + (0, (base + (j + 1) * tn) // LANE)

    in_specs = [pl.BlockSpec((None, tm, K), lambda g, j, i: (g, i, 0))]
    args = [a]
    for w in ws:
        in_specs.append(pl.BlockSpec(sq + gsq + (K, tn), w_main_map))
        args.append(w)
        if shift:
            in_specs.append(pl.BlockSpec(sq + gsq + (K, LANE), w_next_map))
            args.append(w)
    if bias is not None:
        in_specs.append(pl.BlockSpec((None,) * len(bias_lead) + (1, tn),
                                     lambda g, j, i: bias_lead + (0, col_start // tn + j)))
        args.append(bias)
    if scale is not None:
        in_specs.append(pl.BlockSpec((None, tm, 1), lambda g, j, i: (g, i, 0)))
        args.append(scale)
    kern = functools.partial(_mm_kernel, n_w=len(ws), shift=shift, prologue=prologue,
                             epilogue=epilogue, has_bias=bias is not None,
                             has_scale=scale is not None)
    return pl.pallas_call(
        kern,
        grid=(G, n_j, M // tm),
        in_specs=in_specs,
        out_specs=pl.BlockSpec((None, tm, tn), lambda g, j, i: (g, i, j)),
        out_shape=jax.ShapeDtypeStruct((G, M, n_cols), out_dtype),
        scratch_shapes=[pltpu.VMEM((K, tn), BF16) for _ in ws],
        compiler_params=_cparams(3),
        name=name,
    )(*args)


def _merge_kernel(y0, y1, y2, y3, w0, w1, w2, w3, g0, g1, g2, g3, o_ref, b0, b1, b2, b3):
    ys = (y0, y1, y2, y3)
    wsrc = (w0, w1, w2, w3)
    gs = (g0, g1, g2, g3)
    wbf = (b0, b1, b2, b3)

    @pl.when(pl.program_id(1) == 0)
    def _():
        for s, d in zip(wsrc, wbf):
            d[...] = s[...].astype(BF16)

    acc = None
    for k in range(N_BRANCH):
        p = jnp.dot(ys[k][...], wbf[k][...], preferred_element_type=F32) * gs[k][...].astype(F32)
        acc = p if acc is None else acc + p
    o_ref[...] = acc.astype(o_ref.dtype)


def _merge(ys, ws, layer, gates, *, tm=512, tn=512):
    M = ys[0].shape[0]
    n_out = ws[0].shape[-1]
    nj = n_out // tn
    in_specs = [pl.BlockSpec((tm, y.shape[1]), lambda j, i: (i, 0)) for y in ys]
    in_specs += [pl.BlockSpec((None, w.shape[1], tn), lambda j, i: (layer, 0, j)) for w in ws]
    in_specs += [pl.BlockSpec((tm, tn), functools.partial(lambda j, i, k: (i, k * nj + j), k=k))
                 for k in range(N_BRANCH)]
    return pl.pallas_call(
        _merge_kernel,
        grid=(nj, M // tm),
        in_specs=in_specs,
        out_specs=pl.BlockSpec((tm, tn), lambda j, i: (i, j)),
        out_shape=jax.ShapeDtypeStruct((M, n_out), BF16),
        scratch_shapes=[pltpu.VMEM((w.shape[1], tn), BF16) for w in ws],
        compiler_params=_cparams(2),
        name="branch_merge",
    )(*ys, *ws, *([gates] * N_BRANCH))


def _tile_info(i, lay):
    r0 = i * ROW_TILE
    nc_rows = _n_ctx_rows(lay)
    in_ctx = r0 < nc_rows
    rl = jnp.maximum(r0 - nc_rows, 0)
    seg = jnp.where(in_ctx, 0, 1 + rl // lay.lat_len)
    pos = jnp.where(in_ctx, (r0 % lay.ctx_len) // ROW_TILE, (rl % lay.lat_len) // ROW_TILE)
    ntile = jnp.where(in_ctx, lay.ctx_len // ROW_TILE, lay.lat_len // ROW_TILE)
    seq_len = jnp.where(in_ctx, lay.ctx_len, lay.lat_len)
    return seg, pos, ntile, seq_len


def _halo_specs(width, halo, n_rows, col_block=0):
    per = ROW_TILE // halo
    last = n_rows // halo - 1
    cur = pl.BlockSpec((ROW_TILE, width), lambda i: (i, col_block))
    prev = pl.BlockSpec((halo, width), lambda i: (jnp.maximum(i * per - 1, 0), col_block))
    nxt = pl.BlockSpec((halo, width), lambda i: (jnp.minimum((i + 1) * per, last), col_block))
    return [cur, prev, nxt]


def _rows_ahead(x, o):
    if o == 0:
        return x
    return pltpu.roll(x, x.shape[0] - o, 0)


def _extend(prev, cur, nxt, pos, ntile):
    prev = jnp.where(pos > 0, prev, 0.0)
    nxt = jnp.where(pos < ntile - 1, nxt, 0.0)
    return jnp.concatenate([prev, cur, nxt], axis=0)


def _mod_spec(layer, k):
    return pl.BlockSpec((None, MOD_ROWS, D_MODEL), lambda i: (layer, 0, k))


def _modulate_kernel(x_ref, sh_ref, sc_ref, h_ref, *, lay):
    seg = _tile_info(pl.program_id(0), lay)[0]
    sh = sh_ref[pl.ds(seg, 1), :]
    sc = sc_ref[pl.ds(seg, 1), :]
    h_ref[...] = (x_ref[...] * (1.0 + sc) + sh).astype(h_ref.dtype)


def _modulate(x, mods, layer, lay):
    m = x.shape[0]
    return pl.pallas_call(
        functools.partial(_modulate_kernel, lay=lay),
        grid=(m // ROW_TILE,),
        in_specs=[pl.BlockSpec((ROW_TILE, D_MODEL), lambda i: (i, 0)),
                  _mod_spec(layer, 0), _mod_spec(layer, 1)],
        out_specs=pl.BlockSpec((ROW_TILE, D_MODEL), lambda i: (i, 0)),
        out_shape=jax.ShapeDtypeStruct((m, D_MODEL), BF16),
        compiler_params=_cparams(1),
        name="modulate",
    )(x, mods, mods)


def _deepnorm_kernel(*refs, lay, with_router):
    if with_router:
        x_ref, y_ref, g_ref, lng_ref, lnb_ref, sh_ref, sc_ref, wr_ref, xo_ref, h_ref, aff_ref = refs
    else:
        x_ref, y_ref, g_ref, lng_ref, lnb_ref, sh_ref, sc_ref, xo_ref, h_ref = refs
    seg = _tile_info(pl.program_id(0), lay)[0]
    g = g_ref[pl.ds(seg, 1), :]
    v = DEEPNORM_ALPHA * x_ref[...] + g * y_ref[...].astype(F32)
    mu = jnp.mean(v, axis=-1, keepdims=True)
    d = v - mu
    var = jnp.mean(d * d, axis=-1, keepdims=True)
    xn = d * lax.rsqrt(var + LN_EPS) * lng_ref[...] + lnb_ref[...]
    xo_ref[...] = xn
    sh = sh_ref[pl.ds(seg, 1), :]
    sc = sc_ref[pl.ds(seg, 1), :]
    hb = (xn * (1.0 + sc) + sh).astype(BF16)
    h_ref[...] = hb
    if with_router:
        logits = lax.dot_general(wr_ref[...].astype(BF16), hb, (((1,), (1,)), ((), ())),
                                 preferred_element_type=F32)
        mx = jnp.max(logits, axis=0, keepdims=True)
        e = jnp.exp(logits - mx)
        aff_ref[...] = e / jnp.sum(e, axis=0, keepdims=True)


def _deepnorm(x, y, mods, layer, gate_chunk, ln_g, ln_b, next_layer, next_chunk, lay, w_router_t=None):
    m = x.shape[0]
    row = pl.BlockSpec((ROW_TILE, D_MODEL), lambda i: (i, 0))
    vec = pl.BlockSpec((None, 1, D_MODEL), lambda i: (layer, 0, 0))
    in_specs = [row, row, _mod_spec(layer, gate_chunk), vec, vec,
                _mod_spec(next_layer, next_chunk), _mod_spec(next_layer, next_chunk + 1)]
    args = [x, y, mods, ln_g.reshape(DEPTH, 1, D_MODEL), ln_b.reshape(DEPTH, 1, D_MODEL), mods, mods]
    out_specs = [row, row]
    out_shape = [jax.ShapeDtypeStruct((m, D_MODEL), F32), jax.ShapeDtypeStruct((m, D_MODEL), BF16)]
    with_router = w_router_t is not None
    if with_router:
        in_specs.append(pl.BlockSpec((None, N_EXPERTS, D_MODEL), lambda i: (layer, 0, 0)))
        args.append(w_router_t)
        out_specs.append(pl.BlockSpec((N_EXPERTS, ROW_TILE), lambda i: (0, i)))
        out_shape.append(jax.ShapeDtypeStruct((N_EXPERTS, m), F32))
    return pl.pallas_call(
        functools.partial(_deepnorm_kernel, lay=lay, with_router=with_router),
        grid=(m // ROW_TILE,),
        in_specs=in_specs, out_specs=out_specs, out_shape=out_shape,
        compiler_params=_cparams(1),
        name="deepnorm",
    )(*args)


POOL_HALO = 8


def _pool_kernel(cur_ref, prev_ref, nxt_ref, w_ref, sc_ref, o_ref, *, lay):
    _, pos, ntile, seq_len = _tile_info(pl.program_id(0), lay)
    t = pos * ROW_TILE + lax.broadcasted_iota(jnp.int32, (ROW_TILE, 1), 0)
    for g, w in enumerate(POOL_WINDOWS):
        cs = slice(g * POOL_GROUP, (g + 1) * POOL_GROUP)
        cur = cur_ref[:, cs].astype(F32)
        ext = _extend(prev_ref[:, cs].astype(F32), cur, nxt_ref[:, cs].astype(F32), pos, ntile)
        s, span = ext, 1
        while span < w:
            s = s + _rows_ahead(s, span)
            span *= 2
        lo = (w - 1) // 2
        win = _rows_ahead(s, POOL_HALO - lo)[:ROW_TILE]
        cnt = jnp.minimum(t + w // 2, seq_len - 1) - jnp.maximum(t - lo, 0) + 1
        pooled = win / cnt.astype(F32) - cur
        y = jnp.dot(pooled.astype(BF16), w_ref[g].astype(BF16), preferred_element_type=F32)
        o_ref[:, cs] = (y * sc_ref[:, cs]).astype(o_ref.dtype)


def _pool(u_pool, pool_w, pool_scale, layer, lay):
    m = u_pool.shape[0]
    return pl.pallas_call(
        functools.partial(_pool_kernel, lay=lay),
        grid=(m // ROW_TILE,),
        in_specs=_halo_specs(POOL_WIDTH, POOL_HALO, m) + [
            pl.BlockSpec((None, N_POOL, POOL_GROUP, POOL_GROUP), lambda i: (layer, 0, 0, 0)),
            pl.BlockSpec((None, 1, POOL_WIDTH), lambda i: (layer, 0, 0))],
        out_specs=pl.BlockSpec((ROW_TILE, POOL_WIDTH), lambda i: (i, 0)),
        out_shape=jax.ShapeDtypeStruct((m, POOL_WIDTH), BF16),
        compiler_params=_cparams(1),
        name="pool_mixer",
    )(u_pool, u_pool, u_pool, pool_w, pool_scale.reshape(DEPTH, 1, POOL_WIDTH))


CONF_HALO = 16


def _conformer_kernel(cur_ref, prev_ref, nxt_ref, w_ref, b_ref, g_ref, bb_ref, o_ref, *, lay):
    _, pos, ntile, _ = _tile_info(pl.program_id(0), lay)

    def glu(ref):
        return ref[:, :CONF_WIDTH].astype(F32) * jax.nn.sigmoid(ref[:, CONF_WIDTH:].astype(F32))

    ext = _extend(glu(prev_ref), glu(cur_ref), glu(nxt_ref), pos, ntile)
    base = CONF_HALO - CONF_KERNEL // 2
    acc = jnp.zeros((ROW_TILE, CONF_WIDTH), F32)
    for sub in range(SUBLANE):
        r = _rows_ahead(ext, sub)
        for k in range(CONF_KERNEL):
            o = base + k
            if o % SUBLANE == sub:
                al = o - sub
                acc = acc + r[al:al + ROW_TILE] * w_ref[k:k + 1, :]
    hcv = acc + b_ref[...]
    mu = jnp.mean(hcv, axis=-1, keepdims=True)
    d = hcv - mu
    var = jnp.mean(d * d, axis=-1, keepdims=True)
    o_ref[...] = _silu(d * lax.rsqrt(var + LN_EPS) * g_ref[...] + bb_ref[...]).astype(o_ref.dtype)


def _conformer(u_conf, dw_w, dw_b, ln_g, ln_b, layer, lay):
    m = u_conf.shape[0]
    vec = pl.BlockSpec((None, 1, CONF_WIDTH), lambda i: (layer, 0, 0))
    return pl.pallas_call(
        functools.partial(_conformer_kernel, lay=lay),
        grid=(m // ROW_TILE,),
        in_specs=_halo_specs(2 * CONF_WIDTH, CONF_HALO, m) + [
            pl.BlockSpec((None, CONF_KERNEL, CONF_WIDTH), lambda i: (layer, 0, 0)), vec, vec, vec],
        out_specs=pl.BlockSpec((ROW_TILE, CONF_WIDTH), lambda i: (i, 0)),
        out_shape=jax.ShapeDtypeStruct((m, CONF_WIDTH), BF16),
        compiler_params=_cparams(1),
        name="conformer_conv",
    )(u_conf, u_conf, u_conf, dw_w, dw_b.reshape(DEPTH, 1, CONF_WIDTH),
      ln_g.reshape(DEPTH, 1, CONF_WIDTH), ln_b.reshape(DEPTH, 1, CONF_WIDTH))


SSD_HALO = 8


def _ssd_pre_kernel(cur_ref, prev_ref, nxt_ref, w_ref, b_ref, dt_ref, dtb_ref, o_ref, dtt_ref, *, lay):
    _, pos, ntile, _ = _tile_info(pl.program_id(0), lay)
    ext = _extend(prev_ref[...].astype(F32), cur_ref[...].astype(F32), nxt_ref[...].astype(F32), pos, ntile)
    base = SSD_HALO - (SSD_CONV - 1) // 2
    acc = None
    for k in range(SSD_CONV):
        o = base + k
        sub = o % SUBLANE
        term = _rows_ahead(ext, sub)[o - sub:o - sub + ROW_TILE] * w_ref[k:k + 1, :]
        acc = term if acc is None else acc + term
    o_ref[...] = _silu(acc + b_ref[...]).astype(o_ref.dtype)
    z = dt_ref[...].astype(F32) + dtb_ref[...]
    sp = jnp.maximum(z, 0.0) + jnp.log1p(jnp.exp(-jnp.abs(z)))
    dtt_ref[...] = sp.T


def _ssd_pre(u_xbc, u_dt, conv_w, conv_b, dt_bias, layer, lay, col_tile=1024):
    m = u_xbc.shape[0]
    ncol = SSD_XBC // col_tile
    dtb = jnp.pad(dt_bias.reshape(DEPTH, 1, 2 * SSD_HEADS), ((0, 0), (0, 0), (0, LANE - 2 * SSD_HEADS)))

    def kern(cur_ref, prev_ref, nxt_ref, w_ref, b_ref, dt_ref, dtb_ref, o_ref, dtt_ref):
        _ssd_pre_kernel(cur_ref, prev_ref, nxt_ref, w_ref, b_ref, dt_ref, dtb_ref, o_ref, dtt_ref, lay=lay)

    per = ROW_TILE // SSD_HALO
    last = m // SSD_HALO - 1
    return pl.pallas_call(
        kern,
        grid=(m // ROW_TILE, ncol),
        in_specs=[pl.BlockSpec((ROW_TILE, col_tile), lambda i, j: (i, j)),
                  pl.BlockSpec((SSD_HALO, col_tile), lambda i, j: (jnp.maximum(i * per - 1, 0), j)),
                  pl.BlockSpec((SSD_HALO, col_tile), lambda i, j: (jnp.minimum((i + 1) * per, last), j)),
                  pl.BlockSpec((None, SSD_CONV, col_tile), lambda i, j: (layer, 0, j)),
                  pl.BlockSpec((None, 1, col_tile), lambda i, j: (layer, 0, j)),
                  pl.BlockSpec((ROW_TILE, LANE), lambda i, j: (i, 0)),
                  pl.BlockSpec((None, 1, LANE), lambda i, j: (layer, 0, 0))],
        out_specs=[pl.BlockSpec((ROW_TILE, col_tile), lambda i, j: (i, j)),
                   pl.BlockSpec((LANE, ROW_TILE), lambda i, j: (0, i))],
        out_shape=[jax.ShapeDtypeStruct((m, SSD_XBC), F32), jax.ShapeDtypeStruct((LANE, m), F32)],
        compiler_params=_cparams(2),
        name="ssd_conv",
    )(u_xbc, u_xbc, u_xbc, conv_w, conv_b.reshape(DEPTH, 1, SSD_XBC), u_dt, dtb)


def _ssd_scan_kernel(*refs, direction, has_h0, want_state, n_chunks):
    it = iter(refs)
    x_ref, b_ref, c_ref, dtt_ref, alog_ref = next(it), next(it), next(it), next(it), next(it)
    h0_ref = next(it) if has_h0 else None
    if direction == 1:
        yf_ref, z_ref, dsk_ref, ng_ref = next(it), next(it), next(it), next(it)
    next(it)
    o_ref = next(it)
    hfin_ref = next(it) if want_state else None
    h_ref = next(it)

    ci = pl.program_id(1)

    @pl.when(ci == 0)
    def _():
        if has_h0:
            h_ref[...] = h0_ref[...]
        else:
            h_ref[...] = jnp.zeros_like(h_ref)

    q = SSD_CHUNK
    row0 = direction * SSD_HEADS
    a_col = -jnp.exp(alog_ref[...])
    dtt = dtt_ref[...]
    a_dt = dtt * a_col
    ii = lax.broadcasted_iota(jnp.int32, (q, q), 0)
    jj = lax.broadcasted_iota(jnp.int32, (q, q), 1)
    incl = (jj <= ii) if direction == 0 else (jj >= ii)
    incl_f = jnp.where(incl, 1.0, 0.0).astype(F32)
    ident = jnp.where(ii == jj, 1.0, 0.0).astype(F32)
    nt = (((1,), (1,)), ((), ()))
    cum_row = lax.dot_general(a_dt, incl_f, nt, precision=HIGHEST, preferred_element_type=F32)
    cum_col = lax.dot_general(incl_f, a_dt, nt, precision=HIGHEST, preferred_element_type=F32)
    dt_col = lax.dot_general(ident, dtt, nt, precision=HIGHEST, preferred_element_type=F32)
    last = q - 1 if direction == 0 else 0
    tot_row = cum_col[last:last + 1, :]
    tot_col = cum_row[:, last:last + 1]
    dec_end_col = jnp.exp(tot_row - cum_col)
    dec_in_col = jnp.exp(cum_col)
    chunk_decay = jnp.exp(tot_col)

    def split(v):
        hi = v.astype(BF16)
        return hi, (v - hi.astype(F32)).astype(BF16)

    def spread(pair, expand):
        return (jnp.dot(pair[0], expand, preferred_element_type=F32)
                + jnp.dot(pair[1], expand, preferred_element_type=F32))

    dt_pair, dec_end_pair, dec_in_pair = split(dt_col), split(dec_end_col), split(dec_in_col)

    lane_head =lax.broadcasted_iota(jnp.int32, (LANE, SSD_GROUP_WIDTH), 1) // SSD_HEADDIM
    tab_row = lax.broadcasted_iota(jnp.int32, (LANE, SSD_GROUP_WIDTH), 0)
    lane_head_q = lax.broadcasted_iota(jnp.int32, (q, SSD_GROUP_WIDTH), 1) // SSD_HEADDIM

    for g in range(SSD_GROUPS):
        gs = slice(g * SSD_GROUP_WIDTH, (g + 1) * SSD_GROUP_WIDTH)
        ns = slice(g * SSD_STATE, (g + 1) * SSD_STATE)
        head0 = row0 + g * SSD_GROUP_HEADS
        expand = jnp.where(tab_row == head0 + lane_head, 1.0, 0.0).astype(BF16)
        xg = x_ref[:, gs].astype(F32)
        bg = b_ref[:, ns].astype(BF16)
        cg = c_ref[:, ns].astype(BF16)
        xdt = xg * spread(dt_pair, expand)
        xdt_b = xdt.astype(BF16)
        xdt_end = (xdt * spread(dec_end_pair, expand)).astype(BF16)
        cb = lax.dot_general(cg, bg, nt, preferred_element_type=F32)
        y = jnp.zeros((q, SSD_GROUP_WIDTH), F32)
        for k in range(SSD_GROUP_HEADS):
            hh = head0 + k
            seg = cum_col[:, hh:hh + 1] - cum_row[hh:hh + 1, :]
            gmat = (cb * jnp.exp(jnp.where(incl, seg, MASKED))).astype(BF16)
            y = y + jnp.dot(gmat, jnp.where(lane_head_q == k, xdt_b, jnp.zeros_like(xdt_b)),
                            preferred_element_type=F32)
        hs = slice(g * SSD_GROUP_WIDTH, (g + 1) * SSD_GROUP_WIDTH)
        h_in = h_ref[hs, :]
        y_off = lax.dot_general(cg, h_in.astype(BF16), nt, preferred_element_type=F32)
        y = y + y_off * spread(dec_in_pair, expand)
        s_chunk = lax.dot_general(xdt_end, bg, (((0,), (0,)), ((), ())), preferred_element_type=F32)
        for k in range(SSD_GROUP_HEADS):
            hh = head0 + k
            rs = slice(g * SSD_GROUP_WIDTH + k * SSD_HEADDIM, g * SSD_GROUP_WIDTH + (k + 1) * SSD_HEADDIM)
            h_ref[rs, :] = (h_in[k * SSD_HEADDIM:(k + 1) * SSD_HEADDIM] * chunk_decay[hh:hh + 1, :]
                            + s_chunk[k * SSD_HEADDIM:(k + 1) * SSD_HEADDIM])
        if direction == 0:
            o_ref[:, gs] = y
        else:
            v = (yf_ref[:, gs] + y + xg * dsk_ref[:, gs]) * _silu(z_ref[:, gs].astype(F32))
            v = v * lax.rsqrt(jnp.mean(v * v, axis=-1, keepdims=True) + LN_EPS)
            o_ref[:, gs] = (v * ng_ref[:, gs]).astype(o_ref.dtype)

    if want_state:
        @pl.when(ci == n_chunks - 1)
        def _():
            hfin_ref[...] = h_ref[...]


def _ssd_scan(xbc, dtt, a_log, layer, direction, *, row0, n_seq, seq_len, out_buf, n_rows,
              h0=None, want_state=False, yf=None, z=None, d_skip=None, norm_g=None):
    q = SSD_CHUNK
    nc = seq_len // q
    rb0 = row0 // q

    def rb(s, c):
        return rb0 + s * nc + (c if direction == 0 else nc - 1 - c)

    bcol = SSD_INNER // (SSD_GROUPS * SSD_STATE)
    alog = jnp.pad(a_log.reshape(DEPTH, 2 * SSD_HEADS, 1), ((0, 0), (0, LANE - 2 * SSD_HEADS), (0, 0)))
    in_specs = [pl.BlockSpec((q, SSD_INNER), lambda s, c: (rb(s, c), 0)),
                pl.BlockSpec((q, SSD_GROUPS * SSD_STATE), lambda s, c: (rb(s, c), bcol)),
                pl.BlockSpec((q, SSD_GROUPS * SSD_STATE), lambda s, c: (rb(s, c), bcol + 1)),
                pl.BlockSpec((LANE, q), lambda s, c: (0, rb(s, c))),
                pl.BlockSpec((None, LANE, 1), lambda s, c: (layer, 0, 0))]
    args = [xbc, xbc, xbc, dtt, alog]
    if h0 is not None:
        in_specs.append(pl.BlockSpec((None, None, None, SSD_INNER, SSD_STATE),
                                     lambda s, c: (s, layer, direction, 0, 0)))
        args.append(h0)
    row_spec = pl.BlockSpec((q, SSD_INNER), lambda s, c: (rb(s, c), 0))
    if direction == 1:
        vec = pl.BlockSpec((None, 1, SSD_INNER), lambda s, c: (layer, 0, 0))
        in_specs += [row_spec, row_spec, vec, vec]
        args += [yf, z, d_skip, norm_g]
    out_dtype = F32 if direction == 0 else BF16
    out_shape = [jax.ShapeDtypeStruct((n_rows, SSD_INNER), out_dtype)]
    out_specs = [row_spec]
    aliases = {}
    in_specs.append(pl.BlockSpec(memory_space=pl.ANY))
    if out_buf is None:
        args.append(jnp.zeros((SUBLANE, LANE), F32))
    else:
        aliases = {len(args): 0}
        args.append(out_buf)
    if want_state:
        out_shape.append(jax.ShapeDtypeStruct((n_seq, SSD_INNER, SSD_STATE), F32))
        out_specs.append(pl.BlockSpec((None, SSD_INNER, SSD_STATE), lambda s, c: (s, 0, 0)))
    kern = functools.partial(_ssd_scan_kernel, direction=direction, has_h0=h0 is not None,
                             want_state=want_state, n_chunks=nc)
    res = pl.pallas_call(
        kern,
        grid=(n_seq, nc),
        in_specs=in_specs, out_specs=out_specs, out_shape=out_shape,
        scratch_shapes=[pltpu.VMEM((SSD_INNER, SSD_STATE), F32)],
        input_output_aliases=aliases,
        compiler_params=_cparams(2),
        name=f"ssd_scan_d{direction}",
    )(*args)
    return res if want_state else (res[0], None)


def _ssd(u_z, xbc, dtt, a_log, d_skip, norm_g, state, layer, lay):
    m = u_z.shape[0]
    groups = [dict(row0=0, n_seq=lay.n_ctx, seq_len=lay.ctx_len, h0=None, want_state=True),
              dict(row0=_n_ctx_rows(lay), n_seq=lay.n_lat, seq_len=lay.lat_len, h0=state, want_state=False)]
    dsk = jnp.repeat(d_skip, SSD_HEADDIM, axis=-1).reshape(DEPTH, 1, SSD_INNER)
    ng = norm_g.reshape(DEPTH, 1, SSD_INNER)
    yf, y = None, None
    states = []
    for grp in groups:
        yf, hf = _ssd_scan(xbc, dtt, a_log, layer, 0, out_buf=yf, n_rows=m, **grp)
        states.append(hf)
    for grp in groups:
        y, hb = _ssd_scan(xbc, dtt, a_log, layer, 1, out_buf=y, n_rows=m, yf=yf, z=u_z,
                          d_skip=dsk, norm_g=ng, **grp)
        states.append(hb)
    return y, (states[0], states[2])


def _head_masks(rows):
    lane = lax.broadcasted_iota(jnp.int32, (rows, LANE), 1)
    return [lane // NA_HEADDIM == j for j in range(LANE // NA_HEADDIM)]


def _dense_attn_kernel(q_ref, k_ref, v_ref, alias_ref, o_ref):
    del alias_ref
    rows = q_ref.shape[0]
    masks = _head_masks(rows)
    scale = NA_HEADDIM ** -0.5
    nt = (((1,), (1,)), ((), ()))

    def pair(hp, carry):
        cs = pl.ds(pl.multiple_of(hp * LANE, LANE), LANE)
        q = q_ref[:, cs].astype(BF16)
        k = k_ref[:, cs].astype(BF16)
        v = v_ref[:, cs].astype(BF16)
        acc = jnp.zeros((rows, LANE), F32)
        for msk in masks:
            qm = jnp.where(msk, q, jnp.zeros_like(q))
            s = lax.dot_general(qm, k, nt, preferred_element_type=F32) * scale
            e = jnp.exp(s - jnp.max(s, axis=-1, keepdims=True))
            o = jnp.dot(e.astype(BF16), v, preferred_element_type=F32) / jnp.sum(e, axis=-1, keepdims=True)
            acc = acc + jnp.where(msk, o, 0.0)
        o_ref[:, cs] = acc.astype(o_ref.dtype)
        return carry

    lax.fori_loop(0, NA_WIDTH // LANE, pair, 0)


def _dense_attention(u_qkv, lay):
    m = u_qkv.shape[0]
    L = lay.ctx_len
    return pl.pallas_call(
        _dense_attn_kernel,
        grid=(lay.n_ctx,),
        in_specs=[pl.BlockSpec((L, NA_WIDTH), lambda s: (s, 0)),
                  pl.BlockSpec((L, NA_WIDTH), lambda s: (s, 1)),
                  pl.BlockSpec((L, NA_WIDTH), lambda s: (s, 2)),
                  pl.BlockSpec(memory_space=pl.ANY)],
        out_specs=pl.BlockSpec((L, NA_WIDTH), lambda s: (s, 0)),
        out_shape=jax.ShapeDtypeStruct((m, NA_WIDTH), BF16),
        compiler_params=_cparams(1),
        name="dense_attention",
    )(u_qkv, u_qkv, u_qkv, jnp.zeros((SUBLANE, LANE), F32))


NA_QROWS = 4
NA_BIAS_PAIRS = 2 * NA_ROWS - 2


def _na_kernel(rpb_ref, q_ref, k_ref, v_ref, kc_ref, vc_ref, alias_ref, o_ref, tb_ref, *, n_grid_rows):
    del alias_ref
    hp = pl.program_id(1)
    rb = pl.program_id(2)
    n_dc = 2 * NA_COLS - 1
    heads_per_tile = LANE // NA_HEADDIM

    @pl.when(rb == 0)
    def _():
        qi = lax.broadcasted_iota(jnp.int32, (GRID_W, LANE), 0)
        ln = lax.broadcasted_iota(jnp.int32, (GRID_W, LANE), 1)
        kc = ln % GRID_W
        upper = ln >= GRID_W
        cstart = jnp.clip(qi - NA_COLS // 2, 0, GRID_W - NA_COLS)
        valid = (kc >= cstart) & (kc < cstart + NA_COLS)
        rel = kc - qi + NA_COLS - 1
        for j in range(heads_per_tile):
            head = hp * heads_per_tile + j
            for d in range(NA_BIAS_PAIRS):
                t = jnp.full((GRID_W, LANE), MASKED, F32)
                for dc in range(n_dc):
                    lo = rpb_ref[(head * (2 * NA_ROWS - 1) + d) * n_dc + dc]
                    hi = rpb_ref[(head * (2 * NA_ROWS - 1) + d + 1) * n_dc + dc]
                    t = jnp.where(valid & (rel == dc), jnp.where(upper, hi, lo), t)
                tb_ref[j, d] = t

    masks = _head_masks(GRID_W)
    scale = NA_HEADDIM ** -0.5
    nt = (((1,), (1,)), ((), ()))
    kctx = kc_ref[...].astype(BF16)
    vctx = vc_ref[...].astype(BF16)
    for rr in range(NA_QROWS):
        r = rb * NA_QROWS + rr
        rs = jnp.clip(r - NA_ROWS // 2, 0, n_grid_rows - NA_ROWS)
        q = q_ref[rr * GRID_W:(rr + 1) * GRID_W, :].astype(BF16)
        win = pl.ds(pl.multiple_of(rs * GRID_W, GRID_W), NA_ROWS * GRID_W)
        kw = k_ref[win, :].astype(BF16)
        vw = v_ref[win, :].astype(BF16)
        acc = jnp.zeros((GRID_W, LANE), F32)
        for j, msk in enumerate(masks):
            qm = jnp.where(msk, q, jnp.zeros_like(q))
            d0 = rs - r + NA_ROWS - 1
            bias = jnp.concatenate([tb_ref[j, d0 + 2 * pp] for pp in range(NA_ROWS // 2)], axis=1)
            s_nb = lax.dot_general(qm, kw, nt, preferred_element_type=F32) * scale + bias
            s_cx = lax.dot_general(qm, kctx, nt, preferred_element_type=F32) * scale
            mx = jnp.maximum(jnp.max(s_nb, axis=-1, keepdims=True), jnp.max(s_cx, axis=-1, keepdims=True))
            e_nb = jnp.exp(s_nb - mx)
            e_cx = jnp.exp(s_cx - mx)
            den = jnp.sum(e_nb, axis=-1, keepdims=True) + jnp.sum(e_cx, axis=-1, keepdims=True)
            o = (jnp.dot(e_nb.astype(BF16), vw, preferred_element_type=F32)
                 + jnp.dot(e_cx.astype(BF16), vctx, preferred_element_type=F32)) / den
            acc = acc + jnp.where(msk, o, 0.0)
        o_ref[rr * GRID_W:(rr + 1) * GRID_W, :] = acc.astype(o_ref.dtype)


def _neighborhood_attention(u_qkv, y_buf, cache_k, cache_v, rpb, layer, lay):
    m = u_qkv.shape[0]
    L = lay.lat_len
    n_grid_rows = L // GRID_W
    assert n_grid_rows >= NA_ROWS and n_grid_rows % NA_QROWS == 0
    qrows = NA_QROWS * GRID_W
    q0 = _n_ctx_rows(lay) // qrows
    s0 = _n_ctx_rows(lay) // L
    assert _n_ctx_rows(lay) % L == 0
    n_pairs = NA_WIDTH // LANE
    past = cache_k.shape[2]
    return pl.pallas_call(
        functools.partial(_na_kernel, n_grid_rows=n_grid_rows),
        grid=(lay.n_lat, n_pairs, n_grid_rows // NA_QROWS),
        in_specs=[pl.BlockSpec(memory_space=pltpu.SMEM),
                  pl.BlockSpec((qrows, LANE), lambda b, hp, rb: (q0 + b * (L // qrows) + rb, hp)),
                  pl.BlockSpec((L, LANE), lambda b, hp, rb: (s0 + b, n_pairs + hp)),
                  pl.BlockSpec((L, LANE), lambda b, hp, rb: (s0 + b, 2 * n_pairs + hp)),
                  pl.BlockSpec((None, None, past, LANE), lambda b, hp, rb: (b, layer, 0, hp)),
                  pl.BlockSpec((None, None, past, LANE), lambda b, hp, rb: (b, layer, 0, hp)),
                  pl.BlockSpec(memory_space=pl.ANY)],
        out_specs=pl.BlockSpec((qrows, LANE), lambda b, hp, rb: (q0 + b * (L // qrows) + rb, hp)),
        out_shape=jax.ShapeDtypeStruct((m, NA_WIDTH), BF16),
        scratch_shapes=[pltpu.VMEM((LANE // NA_HEADDIM, NA_BIAS_PAIRS, GRID_W, LANE), F32)],
        input_output_aliases={6: 0},
        compiler_params=_cparams(3),
        name="neighborhood_attention",
    )(rpb[layer].reshape(-1), u_qkv, u_qkv, u_qkv, cache_k, cache_v, y_buf)


def _route_kernel(aff_ref, p_ref, g_ref, sel_ref, rank_ref, *, cap):
    e = pl.program_id(1)
    t = aff_ref.shape[1]

    @pl.when(e == 0)
    def _():
        aff = aff_ref[...]
        bits = pltpu.bitcast(aff, jnp.int32)
        capf = jnp.float32(cap)

        def bisect(_, lohi):
            lo, hi = lohi
            mid = lo + ((hi - lo) >> 1)
            cnt = jnp.sum(jnp.where(bits >= mid, 1.0, 0.0), axis=1, keepdims=True)
            ge = cnt >= capf
            return jnp.where(ge, mid, lo), jnp.where(ge, hi, mid)

        lo0 = jnp.zeros((N_EXPERTS, 1), jnp.int32)
        hi0 = jnp.full((N_EXPERTS, 1), 0x7F800000, jnp.int32)
        thr, _ = lax.fori_loop(0, 31, bisect, (lo0, hi0))
        gt = bits > thr
        eq = bits == thr
        n_gt = jnp.sum(jnp.where(gt, 1.0, 0.0), axis=1, keepdims=True)
        before = (lax.broadcasted_iota(jnp.int32, (t, t), 0) < lax.broadcasted_iota(jnp.int32, (t, t), 1))
        before = jnp.where(before, 1.0, 0.0).astype(BF16)
        eq_before = jnp.dot(jnp.where(eq, 1.0, 0.0).astype(BF16), before, preferred_element_type=F32)
        sel = gt | (eq & (eq_before < capf - n_gt))
        self_ = jnp.where(sel, 1.0, 0.0)
        sel_ref[...] = self_
        rank_ref[...] = jnp.dot(self_.astype(BF16), before, preferred_element_type=F32)

    rank = rank_ref[pl.ds(e, 1), :]
    sel = sel_ref[pl.ds(e, 1), :]
    slot = lax.broadcasted_iota(jnp.int32, (cap, t), 0).astype(F32)
    onehot = (slot == rank) & (sel > 0.5)
    p_ref[...] = jnp.where(onehot, 1.0, 0.0).astype(p_ref.dtype)
    g_ref[...] = jnp.sum(jnp.where(onehot, aff_ref[pl.ds(e, 1), :], 0.0), axis=1, keepdims=True)


def _route(aff_t, *, row0, n_seq, seq_len):
    cap = EC_CAPACITY * seq_len // N_EXPERTS
    s0 = row0 // seq_len
    assert row0 % seq_len == 0
    return pl.pallas_call(
        functools.partial(_route_kernel, cap=cap),
        grid=(n_seq, N_EXPERTS),
        in_specs=[pl.BlockSpec((N_EXPERTS, seq_len), lambda s, e: (0, s0 + s))],
        out_specs=[pl.BlockSpec((None, cap, seq_len), lambda s, e: (s, e, 0)),
                   pl.BlockSpec((None, cap, 1), lambda s, e: (s, e, 0))],
        out_shape=[jax.ShapeDtypeStruct((n_seq, N_EXPERTS * cap, seq_len), BF16),
                   jax.ShapeDtypeStruct((n_seq, N_EXPERTS * cap, 1), F32)],
        scratch_shapes=[pltpu.VMEM((N_EXPERTS, seq_len), F32), pltpu.VMEM((N_EXPERTS, seq_len), F32)],
        compiler_params=_cparams(2),
        name="route",
    )(aff_t)


def _gather_kernel(p_ref, h_ref, alias_ref, o_ref, *, cap, n_e):
    del alias_ref
    res = jnp.dot(p_ref[...], h_ref[...], preferred_element_type=F32).astype(o_ref.dtype)
    for k in range(n_e):
        o_ref[k] = res[k * cap:(k + 1) * cap]


def _gather(p, h, xe_buf, *, row0, n_seq, seq_len, slot0, n_slots, e_blk, td):
    cap = EC_CAPACITY * seq_len // N_EXPERTS
    s0 = row0 // seq_len
    c0 = slot0 // cap
    assert slot0 % cap == 0
    in_specs = [pl.BlockSpec((None, e_blk * cap, seq_len), lambda s, eb, dt: (s, eb, 0)),
                pl.BlockSpec((seq_len, td), lambda s, eb, dt: (s0 + s, dt)),
                pl.BlockSpec(memory_space=pl.ANY)]
    aliases = {}
    if xe_buf is None:
        xe_buf = jnp.zeros((SUBLANE, LANE), F32)
    else:
        aliases = {2: 0}
    return pl.pallas_call(
        functools.partial(_gather_kernel, cap=cap, n_e=e_blk),
        grid=(n_seq, N_EXPERTS // e_blk, D_MODEL // td),
        in_specs=in_specs,
        out_specs=pl.BlockSpec((e_blk, cap, td), lambda s, eb, dt: (eb, c0 + s, dt)),
        out_shape=jax.ShapeDtypeStruct((N_EXPERTS, n_slots, D_MODEL), BF16),
        input_output_aliases=aliases,
        compiler_params=_cparams(3),
        name="expert_gather",
    )(p, h, xe_buf)


def _scatter_kernel(p_ref, y_ref, alias_ref, o_ref, *, cap, n_e):
    del alias_ref
    eb = pl.program_id(2)
    ye = jnp.concatenate([y_ref[k] for k in range(n_e)], axis=0)
    res = lax.dot_general(p_ref[...], ye, (((0,), (0,)), ((), ())), preferred_element_type=F32)

    @pl.when(eb == 0)
    def _():
        o_ref[...] = res

    @pl.when(eb > 0)
    def _():
        o_ref[...] += res


def _scatter(p, ye, out_buf, *, row0, n_seq, seq_len, slot0, n_rows, e_blk, td):
    cap = EC_CAPACITY * seq_len // N_EXPERTS
    s0 = row0 // seq_len
    c0 = slot0 // cap
    in_specs = [pl.BlockSpec((None, e_blk * cap, seq_len), lambda s, dt, eb: (s, eb, 0)),
                pl.BlockSpec((e_blk, cap, td), lambda s, dt, eb: (eb, c0 + s, dt)),
                pl.BlockSpec(memory_space=pl.ANY)]
    aliases = {}
    if out_buf is None:
        out_buf = jnp.zeros((SUBLANE, LANE), F32)
    else:
        aliases = {2: 0}
    return pl.pallas_call(
        functools.partial(_scatter_kernel, cap=cap, n_e=e_blk),
        grid=(n_seq, D_MODEL // td, N_EXPERTS // e_blk),
        in_specs=in_specs,
        out_specs=pl.BlockSpec((seq_len, td), lambda s, dt, eb: (s0 + s, dt)),
        out_shape=jax.ShapeDtypeStruct((n_rows, D_MODEL), F32),
        input_output_aliases=aliases,
        compiler_params=_cparams(3),
        name="expert_scatter",
    )(p, ye, out_buf)


def _expert_ffn(h2, aff_t, w_e_gate, w_e_up, w_e_down, layer, lay):
    m = h2.shape[0]
    cap_c = EC_CAPACITY * lay.ctx_len // N_EXPERTS
    cap_l = EC_CAPACITY * lay.lat_len // N_EXPERTS
    slots_c = lay.n_ctx * cap_c
    n_slots = slots_c + lay.n_lat * cap_l
    ctx = dict(row0=0, n_seq=lay.n_ctx, seq_len=lay.ctx_len)
    lat = dict(row0=_n_ctx_rows(lay), n_seq=lay.n_lat, seq_len=lay.lat_len)
    p_c, g_c = _route(aff_t, **ctx)
    p_l, g_l = _route(aff_t, **lat)
    xe = _gather(p_c, h2, None, slot0=0, n_slots=n_slots, e_blk=N_EXPERTS, td=D_MODEL, **ctx)
    xe = _gather(p_l, h2, xe, slot0=slots_c, n_slots=n_slots, e_blk=4, td=2048, **lat)
    ge = jnp.concatenate([
        jnp.swapaxes(g_c.reshape(lay.n_ctx, N_EXPERTS, cap_c), 0, 1).reshape(N_EXPERTS, slots_c, 1),
        jnp.swapaxes(g_l.reshape(lay.n_lat, N_EXPERTS, cap_l), 0, 1).reshape(N_EXPERTS, lay.n_lat * cap_l, 1)],
        axis=1)
    hid = _matmul(xe, [w_e_gate, w_e_up], lead=(layer,), batched_w=True, col_start=0,
                  n_cols=EXPERT_FF, tm=n_slots, tn=256, out_dtype=BF16, epilogue="swiglu", name="expert_up")
    ye = _matmul(hid, [w_e_down], lead=(layer,), batched_w=True, col_start=0, n_cols=D_MODEL,
                 tm=n_slots, tn=1024, out_dtype=BF16, scale=ge, name="expert_down")
    ffn = _scatter(p_c, ye, None, slot0=0, n_rows=m, e_blk=N_EXPERTS, td=D_MODEL, **ctx)
    ffn = _scatter(p_l, ye, ffn, slot0=slots_c, n_rows=m, e_blk=4, td=1024, **lat)
    return ffn


BIG_TM = 1024
BIG_TN = 512


def _modulation_table(c, c_ctx, w_mod, b_mod):
    cond = jnp.concatenate([c_ctx[None], c, jnp.zeros((MOD_ROWS - 1 - c.shape[0], D_MODEL), F32)], axis=0)
    return jnp.concatenate([
        _matmul(cond[None], [w_mod], lead=(l,), batched_w=False, col_start=0, n_cols=6 * D_MODEL,
                tm=MOD_ROWS, tn=1024, out_dtype=F32, prologue="silu",
                bias=b_mod.reshape(DEPTH, 1, 6 * D_MODEL), bias_lead=(l,), name="mod")
        for l in range(DEPTH)], axis=0)


def kernel(x_prompt, x_sample, cache_k, cache_v, state_ssm, c, c_ctx, w_mod, b_mod, w_in, pool_w, pool_scale, ssd_conv_w, ssd_conv_b, ssd_dt_bias, ssd_a_log, ssd_d, ssd_norm_g, na_rpb, conf_dw_w, conf_dw_b, conf_ln_g, conf_ln_b, w_br_pool, w_br_ssd, w_br_na, w_br_conf, w_out, ln1_g, ln1_b, w_router, w_e_gate, w_e_up, w_e_down, ln2_g, ln2_b):
    lay = FULL
    n_ctx_rows = _n_ctx_rows(lay)
    x = jnp.concatenate([x_prompt.reshape(n_ctx_rows, D_MODEL),
                         x_sample.reshape(lay.n_lat * lay.lat_len, D_MODEL)], axis=0)
    mods = _modulation_table(c, c_ctx, w_mod, b_mod)
    w_router_t = jnp.swapaxes(w_router, 1, 2)
    cache_k2 = cache_k.reshape(DEC_BATCH, DEPTH, PAST_LEN, NA_WIDTH)
    cache_v2 = cache_v.reshape(DEC_BATCH, DEPTH, PAST_LEN, NA_WIDTH)
    state2 = state_ssm.reshape(DEC_BATCH, DEPTH, 2, SSD_INNER, SSD_STATE)

    h = _modulate(x, mods, 0, lay)
    ks, vs, hs = [], [], []
    for l in range(DEPTH):
        def in_proj(sec, out_dtype=F32, epilogue=None, tn=BIG_TN, width=None):
            return _matmul(h[None], [w_in], lead=(l,), batched_w=False, col_start=IN_STARTS[sec],
                           n_cols=width or IN_SIZES[sec], tm=BIG_TM, tn=tn, out_dtype=out_dtype,
                           epilogue=epilogue, name=f"in_proj{sec}")[0]

        u_pool = in_proj(0)
        u_z = in_proj(1, out_dtype=BF16)
        u_xbc = in_proj(2)
        u_dt = in_proj(3, tn=LANE, width=LANE)
        u_qkv = in_proj(4, out_dtype=BF16)
        u_conf = in_proj(5, out_dtype=BF16)
        gates = in_proj(6, out_dtype=BF16, epilogue="sigmoid")

        y_pool = _pool(u_pool, pool_w, pool_scale, l, lay)
        xbc, dtt = _ssd_pre(u_xbc, u_dt, ssd_conv_w, ssd_conv_b, ssd_dt_bias, l, lay)
        y_ssd, (h_f, h_b) = _ssd(u_z, xbc, dtt, ssd_a_log, ssd_d, ssd_norm_g, state2, l, lay)
        y_na = _dense_attention(u_qkv, lay)
        y_na = _neighborhood_attention(u_qkv, y_na, cache_k2, cache_v2, na_rpb, l, lay)
        y_conf = _conformer(u_conf, conf_dw_w, conf_dw_b, conf_ln_g, conf_ln_b, l, lay)
        ks.append(u_qkv[:n_ctx_rows, NA_WIDTH:2 * NA_WIDTH].reshape(BATCH, SEQ, NA_WIDTH).astype(F32))
        vs.append(u_qkv[:n_ctx_rows, 2 * NA_WIDTH:].reshape(BATCH, SEQ, NA_WIDTH).astype(F32))
        hs.append(jnp.stack([h_f, h_b], axis=1))

        merged = _merge([y_pool, y_ssd, y_na, y_conf], [w_br_pool, w_br_ssd, w_br_na, w_br_conf], l, gates)
        mix = _matmul(merged[None], [w_out], lead=(l,), batched_w=False, col_start=0, n_cols=D_MODEL,
                      tm=BIG_TM, tn=BIG_TN, out_dtype=F32, name="out_proj")[0]
        x, h2, aff_t = _deepnorm(x, mix, mods, l, 2, ln1_g, ln1_b, l, 3, lay, w_router_t=w_router_t)
        ffn = _expert_ffn(h2, aff_t, w_e_gate, w_e_up, w_e_down, l, lay)
        nl = min(l + 1, DEPTH - 1)
        x, h = _deepnorm(x, ffn, mods, l, 5, ln2_g, ln2_b, nl, 0, lay)

    new_cache_k = jnp.stack(ks, axis=1).reshape(BATCH, DEPTH, SEQ, NA_HEADS, NA_HEADDIM)
    new_cache_v = jnp.stack(vs, axis=1).reshape(BATCH, DEPTH, SEQ, NA_HEADS, NA_HEADDIM)
    new_state = jnp.stack(hs, axis=1).reshape(BATCH, DEPTH, 2, SSD_HEADS, SSD_HEADDIM, SSD_STATE)
    return (x[:n_ctx_rows].reshape(BATCH, SEQ, D_MODEL), x[n_ctx_rows:].reshape(DEC_BATCH, DEC_SEQ, D_MODEL),
            new_cache_k, new_cache_v, new_state)
```

```python
import collections
import functools

import jax
import jax.numpy as jnp
from jax import lax
from jax.experimental import pallas as pl
from jax.experimental.pallas import tpu as pltpu

F32 = jnp.float32
BF16 = jnp.bfloat16
HIGHEST = lax.Precision.HIGHEST

D_MODEL = 4096
BATCH = 32
SEQ = 256
DEPTH = 4
DEC_BATCH = 2
DEC_SEQ = 2048
PAST_LEN = 256
GRID_W = 64
POOL_WIDTH = 1024
POOL_WINDOWS = (2, 4, 8, 16)
N_POOL = len(POOL_WINDOWS)
POOL_GROUP = POOL_WIDTH // N_POOL
SSD_INNER = 2048
SSD_HEADDIM = 64
SSD_HEADS = SSD_INNER // SSD_HEADDIM
SSD_GROUPS = 4
SSD_GROUP_WIDTH = SSD_INNER // SSD_GROUPS
SSD_GROUP_HEADS = SSD_HEADS // SSD_GROUPS
SSD_STATE = 128
SSD_CONV = 4
SSD_CHUNK = 128
SSD_XBC = SSD_INNER + 2 * SSD_GROUPS * SSD_STATE
NA_HEADS = 16
NA_HEADDIM = 64
NA_WIDTH = NA_HEADS * NA_HEADDIM
NA_ROWS = 8
NA_COLS = 16
CONF_WIDTH = 1024
CONF_KERNEL = 31
N_BRANCH = 4
IN_SIZES = (POOL_WIDTH, SSD_INNER, SSD_XBC, 2 * SSD_HEADS, 3 * NA_WIDTH, 2 * CONF_WIDTH, N_BRANCH * D_MODEL)
IN_STARTS = tuple(sum(IN_SIZES[:i]) for i in range(len(IN_SIZES)))
N_EXPERTS = 16
EXPERT_FF = 1024
EC_CAPACITY = 2
DEEPNORM_ALPHA = (2 * DEPTH) ** 0.25
LN_EPS = 1e-5
MASKED = -1e30

MOD_ROWS = 8
LANE = 128
SUBLANE = 8
VMEM_LIMIT_BYTES = 56 * 1024 * 1024
ROW_TILE = 256

Layout = collections.namedtuple("Layout", "n_ctx ctx_len n_lat lat_len")
FULL = Layout(BATCH, SEQ, DEC_BATCH, DEC_SEQ)


def _n_ctx_rows(lay):
    return lay.n_ctx * lay.ctx_len


def _n_rows(lay):
    return lay.n_ctx * lay.ctx_len + lay.n_lat * lay.lat_len


def _cparams(n_grid):
    return pltpu.CompilerParams(dimension_semantics=("arbitrary",) * n_grid,
                                vmem_limit_bytes=VMEM_LIMIT_BYTES)


def _silu(x):
    return x * jax.nn.sigmoid(x)


TRANSPOSE_CHUNK = 512


def _mm_kernel(*refs, n_w, shift, w_transposed, prologue, epilogue, has_bias, has_scale):
    it = iter(refs)
    a_ref = next(it)
    w_refs = []
    for _ in range(n_w):
        wm = next(it)
        wn = next(it) if shift else None
        w_refs.append((wm, wn))
    bias_ref = next(it) if has_bias else None
    scale_ref = next(it) if has_scale else None
    o_ref = next(it)
    wbf = [next(it) for _ in range(n_w)]

    @pl.when(pl.program_id(2) == 0)
    def _():
        for (wm, wn), dst in zip(w_refs, wbf):
            if not w_transposed:
                dst[...] = wm[...].astype(BF16)
                continue
            k_dim = wm.shape[1]
            for k0 in range(0, k_dim, TRANSPOSE_CHUNK):
                ks = slice(k0, min(k0 + TRANSPOSE_CHUNK, k_dim))
                if wn is None:
                    wt = wm[:, ks]
                else:
                    wt = jnp.concatenate([wm[shift:, ks], wn[:, ks]], axis=0)
                dst[ks, :] = wt.T.astype(BF16)

    a = a_ref[...]
    if prologue == "silu":
        a = _silu(a.astype(F32)).astype(BF16)
    acc = [jnp.dot(a, w[...], preferred_element_type=F32) for w in wbf]
    if epilogue == "swiglu":
        g, u = acc
        r = _silu(g) * u
    else:
        r = acc[0]
        if has_bias:
            r = r + bias_ref[...]
        if epilogue == "sigmoid":
            r = jax.nn.sigmoid(r)
        if has_scale:
            r = r * scale_ref[...]
    o_ref[...] = r.astype(o_ref.dtype)


def _matmul(a, ws, *, lead, batched_w, col_start, n_cols, tm, tn, out_dtype,
            prologue=None, epilogue=None, bias=None, bias_lead=(), scale=None, w_transposed=False,
            name="mm"):
    G, M, K = a.shape
    shift = col_start % tn
    base = col_start - shift
    assert n_cols % tn == 0 and M % tm == 0
    assert shift == 0 or (w_transposed and not batched_w and shift % SUBLANE == 0 and tn % shift == 0)
    n_j = n_cols // tn
    sq = (None,) * len(lead)
    gsq = (None,) if batched_w else ()
    gidx = (lambda g: (g,)) if batched_w else (lambda g: ())

    in_specs = [pl.BlockSpec((None, tm, K), lambda g, j, i: (g, i, 0))]
    args = [a]
    for w in ws:
        if w_transposed:
            in_specs.append(pl.BlockSpec(sq + (tn, K), lambda g, j, i: lead + (base // tn + j, 0)))
            args.append(w)
            if shift:
                in_specs.append(pl.BlockSpec(sq + (shift, K),
                                             lambda g, j, i: lead + ((base + (j + 1) * tn) // shift, 0)))
                args.append(w)
        else:
            in_specs.append(pl.BlockSpec(sq + gsq + (K, tn), lambda g, j, i: lead + gidx(g) + (0, base // tn + j)))
            args.append(w)
    if bias is not None:
        in_specs.append(pl.BlockSpec((None,) * len(bias_lead) + (1, tn),
                                     lambda g, j, i: bias_lead + (0, col_start // tn + j)))
        args.append(bias)
    if scale is not None:
        in_specs.append(pl.BlockSpec((None, tm, 1), lambda g, j, i: (g, i, 0)))
        args.append(scale)
    kern = functools.partial(_mm_kernel, n_w=len(ws), shift=shift, w_transposed=w_transposed,
                             prologue=prologue, epilogue=epilogue, has_bias=bias is not None,
                             has_scale=scale is not None)
    return pl.pallas_call(
        kern,
        grid=(G, n_j, M // tm),
        in_specs=in_specs,
        out_specs=pl.BlockSpec((None, tm, tn), lambda g, j, i: (g, i, j)),
        out_shape=jax.ShapeDtypeStruct((G, M, n_cols), out_dtype),
        scratch_shapes=[pltpu.VMEM((K, tn), BF16) for _ in ws],
        compiler_params=_cparams(3),
        name=name,
    )(*args)


def _merge_kernel(y0, y1, y2, y3, w0, w1, w2, w3, g0, g1, g2, g3, o_ref, b0, b1, b2, b3):
    ys = (y0, y1, y2, y3)
    wsrc = (w0, w1, w2, w3)
    gs = (g0, g1, g2, g3)
    wbf = (b0, b1, b2, b3)

    @pl.when(pl.program_id(1) == 0)
    def _():
        for s, d in zip(wsrc, wbf):
            d[...] = s[...].astype(BF16)

    acc = None
    for k in range(N_BRANCH):
        p = jnp.dot(ys[k][...], wbf[k][...], preferred_element_type=F32) * gs[k][...].astype(F32)
        acc = p if acc is None else acc + p
    o_ref[...] = acc.astype(o_ref.dtype)


def _merge(ys, ws, layer, gates, *, tm=512, tn=512):
    M = ys[0].shape[0]
    n_out = ws[0].shape[-1]
    nj = n_out // tn
    in_specs = [pl.BlockSpec((tm, y.shape[1]), lambda j, i: (i, 0)) for y in ys]
    in_specs += [pl.BlockSpec((None, w.shape[1], tn), lambda j, i: (layer, 0, j)) for w in ws]
    in_specs += [pl.BlockSpec((tm, tn), functools.partial(lambda j, i, k: (i, k * nj + j), k=k))
                 for k in range(N_BRANCH)]
    return pl.pallas_call(
        _merge_kernel,
        grid=(nj, M // tm),
        in_specs=in_specs,
        out_specs=pl.BlockSpec((tm, tn), lambda j, i: (i, j)),
        out_shape=jax.ShapeDtypeStruct((M, n_out), BF16),
        scratch_shapes=[pltpu.VMEM((w.shape[1], tn), BF16) for w in ws],
        compiler_params=_cparams(2),
        name="branch_merge",
    )(*ys, *ws, *([gates] * N_BRANCH))


def _tile_info(i, lay):
    r0 = i * ROW_TILE
    nc_rows = _n_ctx_rows(lay)
    in_ctx = r0 < nc_rows
    rl = jnp.maximum(r0 - nc_rows, 0)
    seg = jnp.where(in_ctx, 0, 1 + rl // lay.lat_len)
    pos = jnp.where(in_ctx, (r0 % lay.ctx_len) // ROW_TILE, (rl % lay.lat_len) // ROW_TILE)
    ntile = jnp.where(in_ctx, lay.ctx_len // ROW_TILE, lay.lat_len // ROW_TILE)
    seq_len = jnp.where(in_ctx, lay.ctx_len, lay.lat_len)
    return seg, pos, ntile, seq_len


def _halo_specs(width, halo, n_rows, col_block=0):
    per = ROW_TILE // halo
    last = n_rows // halo - 1
    cur = pl.BlockSpec((ROW_TILE, width), lambda i: (i, col_block))
    prev = pl.BlockSpec((halo, width), lambda i: (jnp.maximum(i * per - 1, 0), col_block))
    nxt = pl.BlockSpec((halo, width), lambda i: (jnp.minimum((i + 1) * per, last), col_block))
    return [cur, prev, nxt]


def _rows_ahead(x, o):
    if o == 0:
        return x
    return pltpu.roll(x, x.shape[0] - o, 0)


def _extend(prev, cur, nxt, pos, ntile):
    prev = jnp.where(pos > 0, prev, 0.0)
    nxt = jnp.where(pos < ntile - 1, nxt, 0.0)
    return jnp.concatenate([prev, cur, nxt], axis=0)


def _mod_spec(layer, k):
    return pl.BlockSpec((None, MOD_ROWS, D_MODEL), lambda i: (layer, 0, k))


def _modulate_kernel(x_ref, sh_ref, sc_ref, h_ref, *, lay):
    seg = _tile_info(pl.program_id(0), lay)[0]
    sh = sh_ref[pl.ds(seg, 1), :]
    sc = sc_ref[pl.ds(seg, 1), :]
    h_ref[...] = (x_ref[...] * (1.0 + sc) + sh).astype(h_ref.dtype)


def _modulate(x, mods, layer, lay):
    m = x.shape[0]
    return pl.pallas_call(
        functools.partial(_modulate_kernel, lay=lay),
        grid=(m // ROW_TILE,),
        in_specs=[pl.BlockSpec((ROW_TILE, D_MODEL), lambda i: (i, 0)),
                  _mod_spec(layer, 0), _mod_spec(layer, 1)],
        out_specs=pl.BlockSpec((ROW_TILE, D_MODEL), lambda i: (i, 0)),
        out_shape=jax.ShapeDtypeStruct((m, D_MODEL), BF16),
        compiler_params=_cparams(1),
        name="modulate",
    )(x, mods, mods)


def _deepnorm_kernel(*refs, lay, with_router):
    if with_router:
        x_ref, y_ref, g_ref, lng_ref, lnb_ref, sh_ref, sc_ref, wr_ref, xo_ref, h_ref, aff_ref = refs
    else:
        x_ref, y_ref, g_ref, lng_ref, lnb_ref, sh_ref, sc_ref, xo_ref, h_ref = refs
    seg = _tile_info(pl.program_id(0), lay)[0]
    g = g_ref[pl.ds(seg, 1), :]
    v = DEEPNORM_ALPHA * x_ref[...] + g * y_ref[...].astype(F32)
    mu = jnp.mean(v, axis=-1, keepdims=True)
    d = v - mu
    var = jnp.mean(d * d, axis=-1, keepdims=True)
    xn = d * lax.rsqrt(var + LN_EPS) * lng_ref[...] + lnb_ref[...]
    xo_ref[...] = xn
    sh = sh_ref[pl.ds(seg, 1), :]
    sc = sc_ref[pl.ds(seg, 1), :]
    hb = (xn * (1.0 + sc) + sh).astype(BF16)
    h_ref[...] = hb
    if with_router:
        logits = lax.dot_general(wr_ref[...].astype(BF16), hb, (((1,), (1,)), ((), ())),
                                 preferred_element_type=F32)
        mx = jnp.max(logits, axis=0, keepdims=True)
        e = jnp.exp(logits - mx)
        aff_ref[...] = e / jnp.sum(e, axis=0, keepdims=True)


def _deepnorm(x, y, mods, layer, gate_chunk, ln_g, ln_b, next_layer, next_chunk, lay, w_router_t=None):
    m = x.shape[0]
    row = pl.BlockSpec((ROW_TILE, D_MODEL), lambda i: (i, 0))
    vec = pl.BlockSpec((None, 1, D_MODEL), lambda i: (layer, 0, 0))
    in_specs = [row, row, _mod_spec(layer, gate_chunk), vec, vec,
                _mod_spec(next_layer, next_chunk), _mod_spec(next_layer, next_chunk + 1)]
    args = [x, y, mods, ln_g.reshape(DEPTH, 1, D_MODEL), ln_b.reshape(DEPTH, 1, D_MODEL), mods, mods]
    out_specs = [row, row]
    out_shape = [jax.ShapeDtypeStruct((m, D_MODEL), F32), jax.ShapeDtypeStruct((m, D_MODEL), BF16)]
    with_router = w_router_t is not None
    if with_router:
        in_specs.append(pl.BlockSpec((None, N_EXPERTS, D_MODEL), lambda i: (layer, 0, 0)))
        args.append(w_router_t)
        out_specs.append(pl.BlockSpec((N_EXPERTS, ROW_TILE), lambda i: (0, i)))
        out_shape.append(jax.ShapeDtypeStruct((N_EXPERTS, m), F32))
    return pl.pallas_call(
        functools.partial(_deepnorm_kernel, lay=lay, with_router=with_router),
        grid=(m // ROW_TILE,),
        in_specs=in_specs, out_specs=out_specs, out_shape=out_shape,
        compiler_params=_cparams(1),
        name="deepnorm",
    )(*args)


POOL_HALO = 8


def _pool_kernel(cur_ref, prev_ref, nxt_ref, w_ref, sc_ref, o_ref, *, lay):
    _, pos, ntile, seq_len = _tile_info(pl.program_id(0), lay)
    t = pos * ROW_TILE + lax.broadcasted_iota(jnp.int32, (ROW_TILE, 1), 0)
    for g, w in enumerate(POOL_WINDOWS):
        cs = slice(g * POOL_GROUP, (g + 1) * POOL_GROUP)
        cur = cur_ref[:, cs].astype(F32)
        ext = _extend(prev_ref[:, cs].astype(F32), cur, nxt_ref[:, cs].astype(F32), pos, ntile)
        s, span = ext, 1
        while span < w:
            s = s + _rows_ahead(s, span)
            span *= 2
        lo = (w - 1) // 2
        win = _rows_ahead(s, POOL_HALO - lo)[:ROW_TILE]
        cnt = jnp.minimum(t + w // 2, seq_len - 1) - jnp.maximum(t - lo, 0) + 1
        pooled = win / cnt.astype(F32) - cur
        y = jnp.dot(pooled.astype(BF16), w_ref[g].astype(BF16), preferred_element_type=F32)
        o_ref[:, cs] = (y * sc_ref[:, cs]).astype(o_ref.dtype)


def _pool(u_pool, pool_w, pool_scale, layer, lay):
    m = u_pool.shape[0]
    return pl.pallas_call(
        functools.partial(_pool_kernel, lay=lay),
        grid=(m // ROW_TILE,),
        in_specs=_halo_specs(POOL_WIDTH, POOL_HALO, m) + [
            pl.BlockSpec((None, N_POOL, POOL_GROUP, POOL_GROUP), lambda i: (layer, 0, 0, 0)),
            pl.BlockSpec((None, 1, POOL_WIDTH), lambda i: (layer, 0, 0))],
        out_specs=pl.BlockSpec((ROW_TILE, POOL_WIDTH), lambda i: (i, 0)),
        out_shape=jax.ShapeDtypeStruct((m, POOL_WIDTH), BF16),
        compiler_params=_cparams(1),
        name="pool_mixer",
    )(u_pool, u_pool, u_pool, pool_w, pool_scale.reshape(DEPTH, 1, POOL_WIDTH))


CONF_HALO = 16


def _conformer_kernel(cur_ref, prev_ref, nxt_ref, w_ref, b_ref, g_ref, bb_ref, o_ref, *, lay):
    _, pos, ntile, _ = _tile_info(pl.program_id(0), lay)

    def glu(ref):
        return ref[:, :CONF_WIDTH].astype(F32) * jax.nn.sigmoid(ref[:, CONF_WIDTH:].astype(F32))

    ext = _extend(glu(prev_ref), glu(cur_ref), glu(nxt_ref), pos, ntile)
    base = CONF_HALO - CONF_KERNEL // 2
    acc = jnp.zeros((ROW_TILE, CONF_WIDTH), F32)
    for sub in range(SUBLANE):
        r = _rows_ahead(ext, sub)
        for k in range(CONF_KERNEL):
            o = base + k
            if o % SUBLANE == sub:
                al = o - sub
                acc = acc + r[al:al + ROW_TILE] * w_ref[k:k + 1, :]
    hcv = acc + b_ref[...]
    mu = jnp.mean(hcv, axis=-1, keepdims=True)
    d = hcv - mu
    var = jnp.mean(d * d, axis=-1, keepdims=True)
    o_ref[...] = _silu(d * lax.rsqrt(var + LN_EPS) * g_ref[...] + bb_ref[...]).astype(o_ref.dtype)


def _conformer(u_conf, dw_w, dw_b, ln_g, ln_b, layer, lay):
    m = u_conf.shape[0]
    vec = pl.BlockSpec((None, 1, CONF_WIDTH), lambda i: (layer, 0, 0))
    return pl.pallas_call(
        functools.partial(_conformer_kernel, lay=lay),
        grid=(m // ROW_TILE,),
        in_specs=_halo_specs(2 * CONF_WIDTH, CONF_HALO, m) + [
            pl.BlockSpec((None, CONF_KERNEL, CONF_WIDTH), lambda i: (layer, 0, 0)), vec, vec, vec],
        out_specs=pl.BlockSpec((ROW_TILE, CONF_WIDTH), lambda i: (i, 0)),
        out_shape=jax.ShapeDtypeStruct((m, CONF_WIDTH), BF16),
        compiler_params=_cparams(1),
        name="conformer_conv",
    )(u_conf, u_conf, u_conf, dw_w, dw_b.reshape(DEPTH, 1, CONF_WIDTH),
      ln_g.reshape(DEPTH, 1, CONF_WIDTH), ln_b.reshape(DEPTH, 1, CONF_WIDTH))


SSD_HALO = 8


def _ssd_pre_kernel(cur_ref, prev_ref, nxt_ref, w_ref, b_ref, dt_ref, dtb_ref, o_ref, dtt_ref, *, lay):
    _, pos, ntile, _ = _tile_info(pl.program_id(0), lay)
    ext = _extend(prev_ref[...].astype(F32), cur_ref[...].astype(F32), nxt_ref[...].astype(F32), pos, ntile)
    base = SSD_HALO - (SSD_CONV - 1) // 2
    acc = None
    for k in range(SSD_CONV):
        o = base + k
        sub = o % SUBLANE
        term = _rows_ahead(ext, sub)[o - sub:o - sub + ROW_TILE] * w_ref[k:k + 1, :]
        acc = term if acc is None else acc + term
    o_ref[...] = _silu(acc + b_ref[...]).astype(o_ref.dtype)
    z = dt_ref[...].astype(F32) + dtb_ref[...]
    sp = jnp.maximum(z, 0.0) + jnp.log1p(jnp.exp(-jnp.abs(z)))
    dtt_ref[...] = sp.T


def _ssd_pre(u_xbc, u_dt, conv_w, conv_b, dt_bias, layer, lay, col_tile=1024):
    m = u_xbc.shape[0]
    ncol = SSD_XBC // col_tile
    dtb = jnp.pad(dt_bias.reshape(DEPTH, 1, 2 * SSD_HEADS), ((0, 0), (0, 0), (0, LANE - 2 * SSD_HEADS)))

    def kern(cur_ref, prev_ref, nxt_ref, w_ref, b_ref, dt_ref, dtb_ref, o_ref, dtt_ref):
        _ssd_pre_kernel(cur_ref, prev_ref, nxt_ref, w_ref, b_ref, dt_ref, dtb_ref, o_ref, dtt_ref, lay=lay)

    per = ROW_TILE // SSD_HALO
    last = m // SSD_HALO - 1
    return pl.pallas_call(
        kern,
        grid=(m // ROW_TILE, ncol),
        in_specs=[pl.BlockSpec((ROW_TILE, col_tile), lambda i, j: (i, j)),
                  pl.BlockSpec((SSD_HALO, col_tile), lambda i, j: (jnp.maximum(i * per - 1, 0), j)),
                  pl.BlockSpec((SSD_HALO, col_tile), lambda i, j: (jnp.minimum((i + 1) * per, last), j)),
                  pl.BlockSpec((None, SSD_CONV, col_tile), lambda i, j: (layer, 0, j)),
                  pl.BlockSpec((None, 1, col_tile), lambda i, j: (layer, 0, j)),
                  pl.BlockSpec((ROW_TILE, LANE), lambda i, j: (i, 0)),
                  pl.BlockSpec((None, 1, LANE), lambda i, j: (layer, 0, 0))],
        out_specs=[pl.BlockSpec((ROW_TILE, col_tile), lambda i, j: (i, j)),
                   pl.BlockSpec((LANE, ROW_TILE), lambda i, j: (0, i))],
        out_shape=[jax.ShapeDtypeStruct((m, SSD_XBC), F32), jax.ShapeDtypeStruct((LANE, m), F32)],
        compiler_params=_cparams(2),
        name="ssd_conv",
    )(u_xbc, u_xbc, u_xbc, conv_w, conv_b.reshape(DEPTH, 1, SSD_XBC), u_dt, dtb)


def _ssd_scan_kernel(*refs, direction, has_h0, want_state, n_chunks):
    it = iter(refs)
    x_ref, b_ref, c_ref, dtt_ref, alog_ref = next(it), next(it), next(it), next(it), next(it)
    h0_ref = next(it) if has_h0 else None
    if direction == 1:
        yf_ref, z_ref, dsk_ref, ng_ref = next(it), next(it), next(it), next(it)
    next(it)
    o_ref = next(it)
    hfin_ref = next(it) if want_state else None
    h_ref = next(it)

    ci = pl.program_id(1)

    @pl.when(ci == 0)
    def _():
        if has_h0:
            h_ref[...] = h0_ref[...]
        else:
            h_ref[...] = jnp.zeros_like(h_ref)

    q = SSD_CHUNK
    row0 = direction * SSD_HEADS
    a_col = -jnp.exp(alog_ref[...])
    dtt = dtt_ref[...]
    a_dt = dtt * a_col
    ii = lax.broadcasted_iota(jnp.int32, (q, q), 0)
    jj = lax.broadcasted_iota(jnp.int32, (q, q), 1)
    incl = (jj <= ii) if direction == 0 else (jj >= ii)
    incl_f = jnp.where(incl, 1.0, 0.0).astype(F32)
    ident = jnp.where(ii == jj, 1.0, 0.0).astype(F32)
    nt = (((1,), (1,)), ((), ()))
    cum_row = lax.dot_general(a_dt, incl_f, nt, precision=HIGHEST, preferred_element_type=F32)
    cum_col = lax.dot_general(incl_f, a_dt, nt, precision=HIGHEST, preferred_element_type=F32)
    dt_col = lax.dot_general(ident, dtt, nt, precision=HIGHEST, preferred_element_type=F32)
    last = q - 1 if direction == 0 else 0
    tot_row = cum_col[last:last + 1, :]
    tot_col = cum_row[:, last:last + 1]
    dec_end_col = jnp.exp(tot_row - cum_col)
    dec_in_col = jnp.exp(cum_col)
    chunk_decay = jnp.exp(tot_col)

    def split(v):
        hi = v.astype(BF16)
        return hi, (v - hi.astype(F32)).astype(BF16)

    def spread(parts, expand):
        out = None
        for p in parts:
            t = jnp.dot(p, expand, preferred_element_type=F32)
            out = t if out is None else out + t
        return out

    dt_parts = split(dt_col)
    dec_end_parts = (dec_end_col.astype(BF16),)
    dec_in_parts = (dec_in_col.astype(BF16),)

    lane_head = lax.broadcasted_iota(jnp.int32, (LANE, SSD_GROUP_WIDTH), 1) // SSD_HEADDIM
    tab_row = lax.broadcasted_iota(jnp.int32, (LANE, SSD_GROUP_WIDTH), 0)
    heads_per_tile = LANE // SSD_HEADDIM
    tile_head = lax.broadcasted_iota(jnp.int32, (q, LANE), 1) // SSD_HEADDIM

    for g in range(SSD_GROUPS):
        gs = slice(g * SSD_GROUP_WIDTH, (g + 1) * SSD_GROUP_WIDTH)
        ns = slice(g * SSD_STATE, (g + 1) * SSD_STATE)
        head0 = row0 + g * SSD_GROUP_HEADS
        expand = jnp.where(tab_row == head0 + lane_head, 1.0, 0.0).astype(BF16)
        xg = x_ref[:, gs].astype(F32)
        bg = b_ref[:, ns].astype(BF16)
        cg = c_ref[:, ns].astype(BF16)
        xdt = xg * spread(dt_parts, expand)
        xdt_b = xdt.astype(BF16)
        xdt_end = (xdt * spread(dec_end_parts, expand)).astype(BF16)
        cb = lax.dot_general(cg, bg, nt, preferred_element_type=F32)
        y_tiles = []
        for t in range(SSD_GROUP_WIDTH // LANE):
            x_tile = xdt_b[:, t * LANE:(t + 1) * LANE]
            y_tile = None
            for k in range(heads_per_tile):
                hh = head0 + t * heads_per_tile + k
                seg = cum_col[:, hh:hh + 1] - cum_row[hh:hh + 1, :]
                gmat = (cb * jnp.exp(jnp.where(incl, seg, MASKED))).astype(BF16)
                part = jnp.dot(gmat, x_tile, preferred_element_type=F32)
                y_tile = part if y_tile is None else jnp.where(tile_head == k, part, y_tile)
            y_tiles.append(y_tile)
        y = jnp.concatenate(y_tiles, axis=1)
        hs = slice(g * SSD_GROUP_WIDTH, (g + 1) * SSD_GROUP_WIDTH)
        h_in = h_ref[hs, :]
        y_off = lax.dot_general(cg, h_in.astype(BF16), nt, preferred_element_type=F32)
        y = y + y_off * spread(dec_in_parts, expand)
        s_chunk = lax.dot_general(xdt_end, bg, (((0,), (0,)), ((), ())), preferred_element_type=F32)
        for k in range(SSD_GROUP_HEADS):
            hh = head0 + k
            rs = slice(g * SSD_GROUP_WIDTH + k * SSD_HEADDIM, g * SSD_GROUP_WIDTH + (k + 1) * SSD_HEADDIM)
            h_ref[rs, :] = (h_in[k * SSD_HEADDIM:(k + 1) * SSD_HEADDIM] * chunk_decay[hh:hh + 1, :]
                            + s_chunk[k * SSD_HEADDIM:(k + 1) * SSD_HEADDIM])
        if direction == 0:
            o_ref[:, gs] = y
        else:
            v = (yf_ref[:, gs] + y + xg * dsk_ref[:, gs]) * _silu(z_ref[:, gs].astype(F32))
            v = v * lax.rsqrt(jnp.mean(v * v, axis=-1, keepdims=True) + LN_EPS)
            o_ref[:, gs] = (v * ng_ref[:, gs]).astype(o_ref.dtype)

    if want_state:
        @pl.when(ci == n_chunks - 1)
        def _():
            hfin_ref[...] = h_ref[...]


def _ssd_scan(xbc, dtt, a_log, layer, direction, *, row0, n_seq, seq_len, out_buf, n_rows,
              h0=None, want_state=False, yf=None, z=None, d_skip=None, norm_g=None):
    q = SSD_CHUNK
    nc = seq_len // q
    rb0 = row0 // q

    def rb(s, c):
        return rb0 + s * nc + (c if direction == 0 else nc - 1 - c)

    bcol = SSD_INNER // (SSD_GROUPS * SSD_STATE)
    alog = jnp.pad(a_log.reshape(DEPTH, 2 * SSD_HEADS, 1), ((0, 0), (0, LANE - 2 * SSD_HEADS), (0, 0)))
    in_specs = [pl.BlockSpec((q, SSD_INNER), lambda s, c: (rb(s, c), 0)),
                pl.BlockSpec((q, SSD_GROUPS * SSD_STATE), lambda s, c: (rb(s, c), bcol)),
                pl.BlockSpec((q, SSD_GROUPS * SSD_STATE), lambda s, c: (rb(s, c), bcol + 1)),
                pl.BlockSpec((LANE, q), lambda s, c: (0, rb(s, c))),
                pl.BlockSpec((None, LANE, 1), lambda s, c: (layer, 0, 0))]
    args = [xbc, xbc, xbc, dtt, alog]
    if h0 is not None:
        in_specs.append(pl.BlockSpec((None, None, None, SSD_INNER, SSD_STATE),
                                     lambda s, c: (s, layer, direction, 0, 0)))
        args.append(h0)
    row_spec = pl.BlockSpec((q, SSD_INNER), lambda s, c: (rb(s, c), 0))
    if direction == 1:
        vec = pl.BlockSpec((None, 1, SSD_INNER), lambda s, c: (layer, 0, 0))
        in_specs += [row_spec, row_spec, vec, vec]
        args += [yf, z, d_skip, norm_g]
    out_dtype = F32 if direction == 0 else BF16
    out_shape = [jax.ShapeDtypeStruct((n_rows, SSD_INNER), out_dtype)]
    out_specs = [row_spec]
    aliases = {}
    in_specs.append(pl.BlockSpec(memory_space=pl.ANY))
    if out_buf is None:
        args.append(jnp.zeros((SUBLANE, LANE), F32))
    else:
        aliases = {len(args): 0}
        args.append(out_buf)
    if want_state:
        out_shape.append(jax.ShapeDtypeStruct((n_seq, SSD_INNER, SSD_STATE), F32))
        out_specs.append(pl.BlockSpec((None, SSD_INNER, SSD_STATE), lambda s, c: (s, 0, 0)))
    kern = functools.partial(_ssd_scan_kernel, direction=direction, has_h0=h0 is not None,
                             want_state=want_state, n_chunks=nc)
    res = pl.pallas_call(
        kern,
        grid=(n_seq, nc),
        in_specs=in_specs, out_specs=out_specs, out_shape=out_shape,
        scratch_shapes=[pltpu.VMEM((SSD_INNER, SSD_STATE), F32)],
        input_output_aliases=aliases,
        compiler_params=_cparams(2),
        name=f"ssd_scan_d{direction}",
    )(*args)
    return res if want_state else (res[0], None)


def _ssd(u_z, xbc, dtt, a_log, d_skip, norm_g, state, layer, lay):
    m = u_z.shape[0]
    groups = [dict(row0=0, n_seq=lay.n_ctx, seq_len=lay.ctx_len, h0=None, want_state=True),
              dict(row0=_n_ctx_rows(lay), n_seq=lay.n_lat, seq_len=lay.lat_len, h0=state, want_state=False)]
    dsk = jnp.repeat(d_skip, SSD_HEADDIM, axis=-1).reshape(DEPTH, 1, SSD_INNER)
    ng = norm_g.reshape(DEPTH, 1, SSD_INNER)
    yf, y = None, None
    states = []
    for grp in groups:
        yf, hf = _ssd_scan(xbc, dtt, a_log, layer, 0, out_buf=yf, n_rows=m, **grp)
        states.append(hf)
    for grp in groups:
        y, hb = _ssd_scan(xbc, dtt, a_log, layer, 1, out_buf=y, n_rows=m, yf=yf, z=u_z,
                          d_skip=dsk, norm_g=ng, **grp)
        states.append(hb)
    return y, (states[0], states[2])


def _head_masks(rows):
    lane = lax.broadcasted_iota(jnp.int32, (rows, LANE), 1)
    return [lane // NA_HEADDIM == j for j in range(LANE // NA_HEADDIM)]


def _dense_attn_kernel(q_ref, k_ref, v_ref, alias_ref, o_ref):
    del alias_ref
    rows = q_ref.shape[0]
    masks = _head_masks(rows)
    scale = NA_HEADDIM ** -0.5
    nt = (((1,), (1,)), ((), ()))

    def pair(hp, carry):
        cs = pl.ds(pl.multiple_of(hp * LANE, LANE), LANE)
        q = q_ref[:, cs].astype(BF16)
        k = k_ref[:, cs].astype(BF16)
        v = v_ref[:, cs].astype(BF16)
        q2 = jnp.concatenate([jnp.where(msk, q, jnp.zeros_like(q)) for msk in masks], axis=0)
        s = lax.dot_general(q2, k, nt, preferred_element_type=F32) * scale
        e = jnp.exp(s - jnp.max(s, axis=-1, keepdims=True))
        o2 = jnp.dot(e.astype(BF16), v, preferred_element_type=F32) / jnp.sum(e, axis=-1, keepdims=True)
        acc = o2[:rows]
        for j in range(1, len(masks)):
            acc = jnp.where(masks[j], o2[j * rows:(j + 1) * rows], acc)
        o_ref[:, cs] = acc.astype(o_ref.dtype)
        return carry

    lax.fori_loop(0, NA_WIDTH // LANE, pair, 0, unroll=2)


def _dense_attention(u_qkv, lay):
    m = u_qkv.shape[0]
    L = lay.ctx_len
    return pl.pallas_call(
        _dense_attn_kernel,
        grid=(lay.n_ctx,),
        in_specs=[pl.BlockSpec((L, NA_WIDTH), lambda s: (s, 0)),
                  pl.BlockSpec((L, NA_WIDTH), lambda s: (s, 1)),
                  pl.BlockSpec((L, NA_WIDTH), lambda s: (s, 2)),
                  pl.BlockSpec(memory_space=pl.ANY)],
        out_specs=pl.BlockSpec((L, NA_WIDTH), lambda s: (s, 0)),
        out_shape=jax.ShapeDtypeStruct((m, NA_WIDTH), BF16),
        compiler_params=_cparams(1),
        name="dense_attention",
    )(u_qkv, u_qkv, u_qkv, jnp.zeros((SUBLANE, LANE), F32))


NA_QROWS = 4
NA_BIAS_PAIRS = 2 * NA_ROWS - 2


def _na_kernel(rpb_ref, q_ref, k_ref, v_ref, kc_ref, vc_ref, alias_ref, o_ref, tb_ref, *, n_grid_rows):
    del alias_ref
    hp = pl.program_id(1)
    rb = pl.program_id(2)
    n_dc = 2 * NA_COLS - 1
    heads_per_tile = LANE // NA_HEADDIM

    @pl.when(rb == 0)
    def _():
        qi = lax.broadcasted_iota(jnp.int32, (GRID_W, LANE), 0)
        ln = lax.broadcasted_iota(jnp.int32, (GRID_W, LANE), 1)
        kc = ln % GRID_W
        upper = ln >= GRID_W
        cstart = jnp.clip(qi - NA_COLS // 2, 0, GRID_W - NA_COLS)
        valid = (kc >= cstart) & (kc < cstart + NA_COLS)
        rel = kc - qi + NA_COLS - 1
        for j in range(heads_per_tile):
            head = hp * heads_per_tile + j
            for d in range(NA_BIAS_PAIRS):
                t = jnp.full((GRID_W, LANE), MASKED, F32)
                for dc in range(n_dc):
                    lo = rpb_ref[(head * (2 * NA_ROWS - 1) + d) * n_dc + dc]
                    hi = rpb_ref[(head * (2 * NA_ROWS - 1) + d + 1) * n_dc + dc]
                    t = jnp.where(valid & (rel == dc), jnp.where(upper, hi, lo), t)
                tb_ref[j, d] = t

    masks = _head_masks(GRID_W)
    scale = NA_HEADDIM ** -0.5
    nt = (((1,), (1,)), ((), ()))
    kctx = kc_ref[...].astype(BF16)
    vctx = vc_ref[...].astype(BF16)
    unit = heads_per_tile * GRID_W
    q2s, s_nbs, vws = [], [], []
    for rr in range(NA_QROWS):
        r = rb * NA_QROWS + rr
        rs = jnp.clip(r - NA_ROWS // 2, 0, n_grid_rows - NA_ROWS)
        q = q_ref[rr * GRID_W:(rr + 1) * GRID_W, :].astype(BF16)
        win = pl.ds(pl.multiple_of(rs * GRID_W, GRID_W), NA_ROWS * GRID_W)
        kw = k_ref[win, :].astype(BF16)
        vws.append(v_ref[win, :].astype(BF16))
        q2 = jnp.concatenate([jnp.where(msk, q, jnp.zeros_like(q)) for msk in masks], axis=0)
        d0 = rs - r + NA_ROWS - 1
        bias = jnp.concatenate(
            [jnp.concatenate([tb_ref[j, d0 + 2 * pp] for pp in range(NA_ROWS // 2)], axis=1)
             for j in range(heads_per_tile)], axis=0)
        q2s.append(q2)
        s_nbs.append(lax.dot_general(q2, kw, nt, preferred_element_type=F32) * scale + bias)
    s_nb = jnp.concatenate(s_nbs, axis=0)
    s_cx = lax.dot_general(jnp.concatenate(q2s, axis=0), kctx, nt, preferred_element_type=F32) * scale
    mx = jnp.maximum(jnp.max(s_nb, axis=-1, keepdims=True), jnp.max(s_cx, axis=-1, keepdims=True))
    e_nb = jnp.exp(s_nb - mx)
    e_cx = jnp.exp(s_cx - mx)
    den = jnp.sum(e_nb, axis=-1, keepdims=True) + jnp.sum(e_cx, axis=-1, keepdims=True)
    e_nb = e_nb.astype(BF16)
    o_cx = jnp.dot(e_cx.astype(BF16), vctx, preferred_element_type=F32)
    for rr in range(NA_QROWS):
        us = slice(rr * unit, (rr + 1) * unit)
        o2 = (jnp.dot(e_nb[us], vws[rr], preferred_element_type=F32) + o_cx[us]) / den[us]
        acc = o2[:GRID_W]
        for j in range(1, heads_per_tile):
            acc = jnp.where(masks[j], o2[j * GRID_W:(j + 1) * GRID_W], acc)
        o_ref[rr * GRID_W:(rr + 1) * GRID_W, :] = acc.astype(o_ref.dtype)


def _neighborhood_attention(u_qkv, y_buf, cache_k, cache_v, rpb, layer, lay):
    m = u_qkv.shape[0]
    L = lay.lat_len
    n_grid_rows = L // GRID_W
    assert n_grid_rows >= NA_ROWS and n_grid_rows % NA_QROWS == 0
    qrows = NA_QROWS * GRID_W
    q0 = _n_ctx_rows(lay) // qrows
    s0 = _n_ctx_rows(lay) // L
    assert _n_ctx_rows(lay) % L == 0
    n_pairs = NA_WIDTH // LANE
    past = cache_k.shape[2]
    return pl.pallas_call(
        functools.partial(_na_kernel, n_grid_rows=n_grid_rows),
        grid=(lay.n_lat, n_pairs, n_grid_rows // NA_QROWS),
        in_specs=[pl.BlockSpec(memory_space=pltpu.SMEM),
                  pl.BlockSpec((qrows, LANE), lambda b, hp, rb: (q0 + b * (L // qrows) + rb, hp)),
                  pl.BlockSpec((L, LANE), lambda b, hp, rb: (s0 + b, n_pairs + hp)),
                  pl.BlockSpec((L, LANE), lambda b, hp, rb: (s0 + b, 2 * n_pairs + hp)),
                  pl.BlockSpec((None, None, past, LANE), lambda b, hp, rb: (b, layer, 0, hp)),
                  pl.BlockSpec((None, None, past, LANE), lambda b, hp, rb: (b, layer, 0, hp)),
                  pl.BlockSpec(memory_space=pl.ANY)],
        out_specs=pl.BlockSpec((qrows, LANE), lambda b, hp, rb: (q0 + b * (L // qrows) + rb, hp)),
        out_shape=jax.ShapeDtypeStruct((m, NA_WIDTH), BF16),
        scratch_shapes=[pltpu.VMEM((LANE // NA_HEADDIM, NA_BIAS_PAIRS, GRID_W, LANE), F32)],
        input_output_aliases={6: 0},
        compiler_params=_cparams(3),
        name="neighborhood_attention",
    )(rpb[layer].reshape(-1), u_qkv, u_qkv, u_qkv, cache_k, cache_v, y_buf)


def _route_kernel(aff_ref, p_ref, g_ref, sel_ref, rank_ref, *, cap, e_blk):
    eb = pl.program_id(1)
    t = aff_ref.shape[1]

    @pl.when(eb == 0)
    def _():
        aff = aff_ref[...]
        bits = pltpu.bitcast(aff, jnp.int32)
        capf = jnp.float32(cap)

        def bisect(_, lohi):
            lo, hi = lohi
            mid = lo + ((hi - lo) >> 1)
            cnt = jnp.sum(jnp.where(bits >= mid, 1.0, 0.0), axis=1, keepdims=True)
            ge = cnt >= capf
            return jnp.where(ge, mid, lo), jnp.where(ge, hi, mid)

        lo0 = jnp.zeros((N_EXPERTS, 1), jnp.int32)
        hi0 = jnp.full((N_EXPERTS, 1), 0x7F800000, jnp.int32)
        thr, _ = lax.fori_loop(0, 31, bisect, (lo0, hi0))
        gt = bits > thr
        eq = bits == thr
        n_gt = jnp.sum(jnp.where(gt, 1.0, 0.0), axis=1, keepdims=True)
        before = (lax.broadcasted_iota(jnp.int32, (t, t), 0) < lax.broadcasted_iota(jnp.int32, (t, t), 1))
        before = jnp.where(before, 1.0, 0.0).astype(BF16)
        eq_before = jnp.dot(jnp.where(eq, 1.0, 0.0).astype(BF16), before, preferred_element_type=F32)
        sel = gt | (eq & (eq_before < capf - n_gt))
        self_ = jnp.where(sel, 1.0, 0.0)
        sel_ref[...] = self_
        rank_ref[...] = jnp.dot(self_.astype(BF16), before, preferred_element_type=F32)

    slot = lax.broadcasted_iota(jnp.int32, (cap, t), 0).astype(F32)
    for k in range(e_blk):
        e = eb * e_blk + k
        rank = rank_ref[pl.ds(e, 1), :]
        sel = sel_ref[pl.ds(e, 1), :]
        onehot = (slot == rank) & (sel > 0.5)
        p_ref[k * cap:(k + 1) * cap, :] = jnp.where(onehot, 1.0, 0.0).astype(p_ref.dtype)
        g_ref[k * cap:(k + 1) * cap, :] = jnp.sum(jnp.where(onehot, aff_ref[pl.ds(e, 1), :], 0.0),
                                                  axis=1, keepdims=True)


def _route(aff_t, *, row0, n_seq, seq_len, e_blk):
    cap = EC_CAPACITY * seq_len // N_EXPERTS
    s0 = row0 // seq_len
    assert row0 % seq_len == 0
    return pl.pallas_call(
        functools.partial(_route_kernel, cap=cap, e_blk=e_blk),
        grid=(n_seq, N_EXPERTS // e_blk),
        in_specs=[pl.BlockSpec((N_EXPERTS, seq_len), lambda s, e: (0, s0 + s))],
        out_specs=[pl.BlockSpec((None, e_blk * cap, seq_len), lambda s, e: (s, e, 0)),
                   pl.BlockSpec((None, e_blk * cap, 1), lambda s, e: (s, e, 0))],
        out_shape=[jax.ShapeDtypeStruct((n_seq, N_EXPERTS * cap, seq_len), BF16),
                   jax.ShapeDtypeStruct((n_seq, N_EXPERTS * cap, 1), F32)],
        scratch_shapes=[pltpu.VMEM((N_EXPERTS, seq_len), F32), pltpu.VMEM((N_EXPERTS, seq_len), F32)],
        compiler_params=_cparams(2),
        name="route",
    )(aff_t)


def _gather_kernel(p_ref, h_ref, alias_ref, o_ref, *, cap, n_e):
    del alias_ref
    res = jnp.dot(p_ref[...], h_ref[...], preferred_element_type=F32).astype(o_ref.dtype)
    for k in range(n_e):
        o_ref[k] = res[k * cap:(k + 1) * cap]


def _gather(p, h, xe_buf, *, row0, n_seq, seq_len, slot0, n_slots, e_blk, td):
    cap = EC_CAPACITY * seq_len // N_EXPERTS
    s0 = row0 // seq_len
    c0 = slot0 // cap
    assert slot0 % cap == 0
    in_specs = [pl.BlockSpec((None, e_blk * cap, seq_len), lambda s, eb, dt: (s, eb, 0)),
                pl.BlockSpec((seq_len, td), lambda s, eb, dt: (s0 + s, dt)),
                pl.BlockSpec(memory_space=pl.ANY)]
    aliases = {}
    if xe_buf is None:
        xe_buf = jnp.zeros((SUBLANE, LANE), F32)
    else:
        aliases = {2: 0}
    return pl.pallas_call(
        functools.partial(_gather_kernel, cap=cap, n_e=e_blk),
        grid=(n_seq, N_EXPERTS // e_blk, D_MODEL // td),
        in_specs=in_specs,
        out_specs=pl.BlockSpec((e_blk, cap, td), lambda s, eb, dt: (eb, c0 + s, dt)),
        out_shape=jax.ShapeDtypeStruct((N_EXPERTS, n_slots, D_MODEL), BF16),
        input_output_aliases=aliases,
        compiler_params=_cparams(3),
        name="expert_gather",
    )(p, h, xe_buf)


def _scatter_kernel(p_ref, y_ref, alias_ref, o_ref, *, cap, n_e):
    del alias_ref
    eb = pl.program_id(2)
    ye = jnp.concatenate([y_ref[k] for k in range(n_e)], axis=0)
    res = lax.dot_general(p_ref[...], ye, (((0,), (0,)), ((), ())), preferred_element_type=F32)

    @pl.when(eb == 0)
    def _():
        o_ref[...] = res

    @pl.when(eb > 0)
    def _():
        o_ref[...] += res


def _scatter(p, ye, out_buf, *, row0, n_seq, seq_len, slot0, n_rows, e_blk, td):
    cap = EC_CAPACITY * seq_len // N_EXPERTS
    s0 = row0 // seq_len
    c0 = slot0 // cap
    in_specs = [pl.BlockSpec((None, e_blk * cap, seq_len), lambda s, dt, eb: (s, eb, 0)),
                pl.BlockSpec((e_blk, cap, td), lambda s, dt, eb: (eb, c0 + s, dt)),
                pl.BlockSpec(memory_space=pl.ANY)]
    aliases = {}
    if out_buf is None:
        out_buf = jnp.zeros((SUBLANE, LANE), F32)
    else:
        aliases = {2: 0}
    return pl.pallas_call(
        functools.partial(_scatter_kernel, cap=cap, n_e=e_blk),
        grid=(n_seq, D_MODEL // td, N_EXPERTS // e_blk),
        in_specs=in_specs,
        out_specs=pl.BlockSpec((seq_len, td), lambda s, dt, eb: (s0 + s, dt)),
        out_shape=jax.ShapeDtypeStruct((n_rows, D_MODEL), F32),
        input_output_aliases=aliases,
        compiler_params=_cparams(3),
        name="expert_scatter",
    )(p, ye, out_buf)


def _expert_ffn(h2, aff_t, w_e_gate, w_e_up, w_e_down, layer, lay):
    m = h2.shape[0]
    cap_c = EC_CAPACITY * lay.ctx_len // N_EXPERTS
    cap_l = EC_CAPACITY * lay.lat_len // N_EXPERTS
    slots_c = lay.n_ctx * cap_c
    n_slots = slots_c + lay.n_lat * cap_l
    ctx = dict(row0=0, n_seq=lay.n_ctx, seq_len=lay.ctx_len)
    lat = dict(row0=_n_ctx_rows(lay), n_seq=lay.n_lat, seq_len=lay.lat_len)
    p_c, g_c = _route(aff_t, e_blk=N_EXPERTS, **ctx)
    p_l, g_l = _route(aff_t, e_blk=1, **lat)
    xe = _gather(p_c, h2, None, slot0=0, n_slots=n_slots, e_blk=N_EXPERTS, td=D_MODEL, **ctx)
    xe = _gather(p_l, h2, xe, slot0=slots_c, n_slots=n_slots, e_blk=4, td=2048, **lat)
    ge = jnp.concatenate([
        jnp.swapaxes(g_c.reshape(lay.n_ctx, N_EXPERTS, cap_c), 0, 1).reshape(N_EXPERTS, slots_c, 1),
        jnp.swapaxes(g_l.reshape(lay.n_lat, N_EXPERTS, cap_l), 0, 1).reshape(N_EXPERTS, lay.n_lat * cap_l, 1)],
        axis=1)
    hid = _matmul(xe, [w_e_gate, w_e_up], lead=(layer,), batched_w=True, col_start=0,
                  n_cols=EXPERT_FF, tm=n_slots, tn=256, out_dtype=BF16, epilogue="swiglu", name="expert_up")
    ye = _matmul(hid, [w_e_down], lead=(layer,), batched_w=True, col_start=0, n_cols=D_MODEL,
                 tm=n_slots, tn=1024, out_dtype=BF16, scale=ge, name="expert_down")
    ffn = _scatter(p_c, ye, None, slot0=0, n_rows=m, e_blk=N_EXPERTS, td=D_MODEL, **ctx)
    ffn = _scatter(p_l, ye, ffn, slot0=slots_c, n_rows=m, e_blk=4, td=1024, **lat)
    return ffn


BIG_TM = 1024
BIG_TN = 512


def _modulation_table(c, c_ctx, w_mod, b_mod):
    cond = jnp.concatenate([c_ctx[None], c, jnp.zeros((MOD_ROWS - 1 - c.shape[0], D_MODEL), F32)], axis=0)
    return jnp.concatenate([
        _matmul(cond[None], [w_mod], lead=(l,), batched_w=False, col_start=0, n_cols=6 * D_MODEL,
                tm=MOD_ROWS, tn=1024, out_dtype=F32, prologue="silu",
                bias=b_mod.reshape(DEPTH, 1, 6 * D_MODEL), bias_lead=(l,), name="mod")
        for l in range(DEPTH)], axis=0)


def kernel(x_prompt, x_sample, cache_k, cache_v, state_ssm, c, c_ctx, w_mod, b_mod, w_in, pool_w, pool_scale, ssd_conv_w, ssd_conv_b, ssd_dt_bias, ssd_a_log, ssd_d, ssd_norm_g, na_rpb, conf_dw_w, conf_dw_b, conf_ln_g, conf_ln_b, w_br_pool, w_br_ssd, w_br_na, w_br_conf, w_out, ln1_g, ln1_b, w_router, w_e_gate, w_e_up, w_e_down, ln2_g, ln2_b):
    lay = FULL
    n_ctx_rows = _n_ctx_rows(lay)
    x = jnp.concatenate([x_prompt.reshape(n_ctx_rows, D_MODEL),
                         x_sample.reshape(lay.n_lat * lay.lat_len, D_MODEL)], axis=0)
    mods = _modulation_table(c, c_ctx, w_mod, b_mod)
    w_router_t = jnp.swapaxes(w_router, 1, 2)
    w_in_t = jnp.swapaxes(w_in, 1, 2)
    cache_k2 = cache_k.reshape(DEC_BATCH, DEPTH, PAST_LEN, NA_WIDTH)
    cache_v2 = cache_v.reshape(DEC_BATCH, DEPTH, PAST_LEN, NA_WIDTH)
    state2 = state_ssm.reshape(DEC_BATCH, DEPTH, 2, SSD_INNER, SSD_STATE)

    h = _modulate(x, mods, 0, lay)
    ks, vs, hs = [], [], []
    for l in range(DEPTH):
        def in_proj(sec, out_dtype=F32, epilogue=None, tn=BIG_TN, width=None):
            return _matmul(h[None], [w_in_t], lead=(l,), batched_w=False, col_start=IN_STARTS[sec],
                           n_cols=width or IN_SIZES[sec], tm=BIG_TM, tn=tn, out_dtype=out_dtype,
                           epilogue=epilogue, w_transposed=True, name=f"in_proj{sec}")[0]

        u_pool = in_proj(0)
        u_z = in_proj(1, out_dtype=BF16)
        u_xbc = in_proj(2)
        u_dt = in_proj(3, tn=LANE, width=LANE)
        u_qkv = in_proj(4, out_dtype=BF16)
        u_conf = in_proj(5, out_dtype=BF16)
        gates = in_proj(6, out_dtype=BF16, epilogue="sigmoid")

        y_pool = _pool(u_pool, pool_w, pool_scale, l, lay)
        xbc, dtt = _ssd_pre(u_xbc, u_dt, ssd_conv_w, ssd_conv_b, ssd_dt_bias, l, lay)
        y_ssd, (h_f, h_b) = _ssd(u_z, xbc, dtt, ssd_a_log, ssd_d, ssd_norm_g, state2, l, lay)
        y_na = _dense_attention(u_qkv, lay)
        y_na = _neighborhood_attention(u_qkv, y_na, cache_k2, cache_v2, na_rpb, l, lay)
        y_conf = _conformer(u_conf, conf_dw_w, conf_dw_b, conf_ln_g, conf_ln_b, l, lay)
        ks.append(u_qkv[:n_ctx_rows, NA_WIDTH:2 * NA_WIDTH].reshape(BATCH, SEQ, NA_WIDTH).astype(F32))
        vs.append(u_qkv[:n_ctx_rows, 2 * NA_WIDTH:].reshape(BATCH, SEQ, NA_WIDTH).astype(F32))
        hs.append(jnp.stack([h_f, h_b], axis=1))

        merged = _merge([y_pool, y_ssd, y_na, y_conf], [w_br_pool, w_br_ssd, w_br_na, w_br_conf], l, gates)
        mix = _matmul(merged[None], [w_out], lead=(l,), batched_w=False, col_start=0, n_cols=D_MODEL,
                      tm=BIG_TM, tn=BIG_TN, out_dtype=BF16, name="out_proj")[0]
        x, h2, aff_t = _deepnorm(x, mix, mods, l, 2, ln1_g, ln1_b, l, 3, lay, w_router_t=w_router_t)
        ffn = _expert_ffn(h2, aff_t, w_e_gate, w_e_up, w_e_down, l, lay)
        nl = min(l + 1, DEPTH - 1)
        x, h = _deepnorm(x, ffn, mods, l, 5, ln2_g, ln2_b, nl, 0, lay)

    new_cache_k = jnp.stack(ks, axis=1).reshape(BATCH, DEPTH, SEQ, NA_HEADS, NA_HEADDIM)
    new_cache_v = jnp.stack(vs, axis=1).reshape(BATCH, DEPTH, SEQ, NA_HEADS, NA_HEADDIM)
    new_state = jnp.stack(hs, axis=1).reshape(BATCH, DEPTH, 2, SSD_HEADS, SSD_HEADDIM, SSD_STATE)
    return (x[:n_ctx_rows].reshape(BATCH, SEQ, D_MODEL), x[n_ctx_rows:].reshape(DEC_BATCH, DEC_SEQ, D_MODEL),
            new_cache_k, new_cache_v, new_state)
```

```python
import collections
import functools

import jax
import jax.numpy as jnp
from jax import lax
from jax.experimental import pallas as pl
from jax.experimental.pallas import tpu as pltpu

F32 = jnp.float32
BF16 = jnp.bfloat16
HIGHEST = lax.Precision.HIGHEST

D_MODEL = 4096
BATCH = 32
SEQ = 256
DEPTH = 4
DEC_BATCH = 2
DEC_SEQ = 2048
PAST_LEN = 256
GRID_W = 64
POOL_WIDTH = 1024
POOL_WINDOWS = (2, 4, 8, 16)
N_POOL = len(POOL_WINDOWS)
POOL_GROUP = POOL_WIDTH // N_POOL
SSD_INNER = 2048
SSD_HEADDIM = 64
SSD_HEADS = SSD_INNER // SSD_HEADDIM
SSD_GROUPS = 4
SSD_GROUP_WIDTH = SSD_INNER // SSD_GROUPS
SSD_GROUP_HEADS = SSD_HEADS // SSD_GROUPS
SSD_STATE = 128
SSD_CONV = 4
SSD_CHUNK = 128
SSD_XBC = SSD_INNER + 2 * SSD_GROUPS * SSD_STATE
NA_HEADS = 16
NA_HEADDIM = 64
NA_WIDTH = NA_HEADS * NA_HEADDIM
NA_ROWS = 8
NA_COLS = 16
CONF_WIDTH = 1024
CONF_KERNEL = 31
N_BRANCH = 4
IN_SIZES = (POOL_WIDTH, SSD_INNER, SSD_XBC, 2 * SSD_HEADS, 3 * NA_WIDTH, 2 * CONF_WIDTH, N_BRANCH * D_MODEL)
IN_STARTS = tuple(sum(IN_SIZES[:i]) for i in range(len(IN_SIZES)))
N_EXPERTS = 16
EXPERT_FF = 1024
EC_CAPACITY = 2
DEEPNORM_ALPHA = (2 * DEPTH) ** 0.25
LN_EPS = 1e-5
MASKED = -1e30

MOD_ROWS = 8
LANE = 128
SUBLANE = 8
VMEM_LIMIT_BYTES = 56 * 1024 * 1024
ROW_TILE = 256

Layout = collections.namedtuple("Layout", "n_ctx ctx_len n_lat lat_len")
FULL = Layout(BATCH, SEQ, DEC_BATCH, DEC_SEQ)


def _n_ctx_rows(lay):
    return lay.n_ctx * lay.ctx_len


def _n_rows(lay):
    return lay.n_ctx * lay.ctx_len + lay.n_lat * lay.lat_len


def _cparams(n_grid):
    return pltpu.CompilerParams(dimension_semantics=("arbitrary",) * n_grid,
                                vmem_limit_bytes=VMEM_LIMIT_BYTES)


def _silu(x):
    return x * jax.nn.sigmoid(x)


TRANSPOSE_CHUNK = 512


def _mm_kernel(*refs, n_w, shift, w_transposed, prologue, epilogue, has_bias, has_scale):
    it = iter(refs)
    a_ref = next(it)
    w_refs = []
    for _ in range(n_w):
        wm = next(it)
        wn = next(it) if shift else None
        w_refs.append((wm, wn))
    bias_ref = next(it) if has_bias else None
    scale_ref = next(it) if has_scale else None
    o_ref = next(it)
    wbf = [next(it) for _ in range(n_w)]

    @pl.when(pl.program_id(2) == 0)
    def _():
        for (wm, wn), dst in zip(w_refs, wbf):
            if not w_transposed:
                dst[...] = wm[...].astype(BF16)
                continue
            k_dim = wm.shape[1]
            for k0 in range(0, k_dim, TRANSPOSE_CHUNK):
                ks = slice(k0, min(k0 + TRANSPOSE_CHUNK, k_dim))
                if wn is None:
                    wt = wm[:, ks]
                else:
                    wt = jnp.concatenate([wm[shift:, ks], wn[:, ks]], axis=0)
                dst[ks, :] = wt.T.astype(BF16)

    a = a_ref[...]
    if prologue == "silu":
        a = _silu(a.astype(F32)).astype(BF16)
    acc = [jnp.dot(a, w[...], preferred_element_type=F32) for w in wbf]
    if epilogue == "swiglu":
        g, u = acc
        r = _silu(g) * u
    else:
        r = acc[0]
        if has_bias:
            r = r + bias_ref[...]
        if epilogue == "sigmoid":
            r = 0.5 * jnp.tanh(0.5 * r) + 0.5
        if has_scale:
            r = r * scale_ref[...]
    o_ref[...] = r.astype(o_ref.dtype)


def _matmul(a, ws, *, lead, batched_w, col_start, n_cols, tm, tn, out_dtype,
            prologue=None, epilogue=None, bias=None, bias_lead=(), scale=None, w_transposed=False,
            name="mm"):
    G, M, K = a.shape
    shift = col_start % tn
    base = col_start - shift
    assert n_cols % tn == 0 and M % tm == 0
    assert shift == 0 or (w_transposed and not batched_w and shift % SUBLANE == 0 and tn % shift == 0)
    n_j = n_cols // tn
    sq = (None,) * len(lead)
    gsq = (None,) if batched_w else ()
    gidx = (lambda g: (g,)) if batched_w else (lambda g: ())

    in_specs = [pl.BlockSpec((None, tm, K), lambda g, j, i: (g, i, 0))]
    args = [a]
    for w in ws:
        if w_transposed:
            in_specs.append(pl.BlockSpec(sq + (tn, K), lambda g, j, i: lead + (base // tn + j, 0)))
            args.append(w)
            if shift:
                in_specs.append(pl.BlockSpec(sq + (shift, K),
                                             lambda g, j, i: lead + ((base + (j + 1) * tn) // shift, 0)))
                args.append(w)
        else:
            in_specs.append(pl.BlockSpec(sq + gsq + (K, tn), lambda g, j, i: lead + gidx(g) + (0, base // tn + j)))
            args.append(w)
    if bias is not None:
        in_specs.append(pl.BlockSpec((None,) * len(bias_lead) + (1, tn),
                                     lambda g, j, i: bias_lead + (0, col_start // tn + j)))
        args.append(bias)
    if scale is not None:
        in_specs.append(pl.BlockSpec((None, tm, 1), lambda g, j, i: (g, i, 0)))
        args.append(scale)
    kern = functools.partial(_mm_kernel, n_w=len(ws), shift=shift, w_transposed=w_transposed,
                             prologue=prologue, epilogue=epilogue, has_bias=bias is not None,
                             has_scale=scale is not None)
    return pl.pallas_call(
        kern,
        grid=(G, n_j, M // tm),
        in_specs=in_specs,
        out_specs=pl.BlockSpec((None, tm, tn), lambda g, j, i: (g, i, j)),
        out_shape=jax.ShapeDtypeStruct((G, M, n_cols), out_dtype),
        scratch_shapes=[pltpu.VMEM((K, tn), BF16) for _ in ws],
        compiler_params=_cparams(3),
        name=name,
    )(*args)


def _merge_kernel(y0, y1, y2, y3, w0, w1, w2, w3, g0, g1, g2, g3, o_ref, b0, b1, b2, b3):
    ys = (y0, y1, y2, y3)
    wsrc = (w0, w1, w2, w3)
    gs = (g0, g1, g2, g3)
    wbf = (b0, b1, b2, b3)

    @pl.when(pl.program_id(1) == 0)
    def _():
        for s, d in zip(wsrc, wbf):
            d[...] = s[...].astype(BF16)

    acc = None
    for k in range(N_BRANCH):
        p = jnp.dot(ys[k][...], wbf[k][...], preferred_element_type=F32) * gs[k][...].astype(F32)
        acc = p if acc is None else acc + p
    o_ref[...] = acc.astype(o_ref.dtype)


def _merge(ys, ws, layer, gates, *, tm=512, tn=512):
    M = ys[0].shape[0]
    n_out = ws[0].shape[-1]
    nj = n_out // tn
    in_specs = [pl.BlockSpec((tm, y.shape[1]), lambda j, i: (i, 0)) for y in ys]
    in_specs += [pl.BlockSpec((None, w.shape[1], tn), lambda j, i: (layer, 0, j)) for w in ws]
    in_specs += [pl.BlockSpec((tm, tn), functools.partial(lambda j, i, k: (i, k * nj + j), k=k))
                 for k in range(N_BRANCH)]
    return pl.pallas_call(
        _merge_kernel,
        grid=(nj, M // tm),
        in_specs=in_specs,
        out_specs=pl.BlockSpec((tm, tn), lambda j, i: (i, j)),
        out_shape=jax.ShapeDtypeStruct((M, n_out), BF16),
        scratch_shapes=[pltpu.VMEM((w.shape[1], tn), BF16) for w in ws],
        compiler_params=_cparams(2),
        name="branch_merge",
    )(*ys, *ws, *([gates] * N_BRANCH))


def _tile_info(i, lay):
    r0 = i * ROW_TILE
    nc_rows = _n_ctx_rows(lay)
    in_ctx = r0 < nc_rows
    rl = jnp.maximum(r0 - nc_rows, 0)
    seg = jnp.where(in_ctx, 0, 1 + rl // lay.lat_len)
    pos = jnp.where(in_ctx, (r0 % lay.ctx_len) // ROW_TILE, (rl % lay.lat_len) // ROW_TILE)
    ntile = jnp.where(in_ctx, lay.ctx_len // ROW_TILE, lay.lat_len // ROW_TILE)
    seq_len = jnp.where(in_ctx, lay.ctx_len, lay.lat_len)
    return seg, pos, ntile, seq_len


def _halo_specs(width, halo, n_rows, col_block=0):
    per = ROW_TILE // halo
    last = n_rows // halo - 1
    cur = pl.BlockSpec((ROW_TILE, width), lambda i: (i, col_block))
    prev = pl.BlockSpec((halo, width), lambda i: (jnp.maximum(i * per - 1, 0), col_block))
    nxt = pl.BlockSpec((halo, width), lambda i: (jnp.minimum((i + 1) * per, last), col_block))
    return [cur, prev, nxt]


def _rows_ahead(x, o):
    if o == 0:
        return x
    return pltpu.roll(x, x.shape[0] - o, 0)


def _extend(prev, cur, nxt, pos, ntile):
    prev = jnp.where(pos > 0, prev, 0.0)
    nxt = jnp.where(pos < ntile - 1, nxt, 0.0)
    return jnp.concatenate([prev, cur, nxt], axis=0)


def _mod_spec(layer, k):
    return pl.BlockSpec((None, MOD_ROWS, D_MODEL), lambda i: (layer, 0, k))


def _x_specs(x, lay):
    if not isinstance(x, tuple):
        return [pl.BlockSpec((ROW_TILE, D_MODEL), lambda i: (i, 0))], [x]
    nct = _n_ctx_rows(lay) // ROW_TILE
    return ([pl.BlockSpec((ROW_TILE, D_MODEL), lambda i: (jnp.minimum(i, nct - 1), 0)),
             pl.BlockSpec((ROW_TILE, D_MODEL), lambda i: (jnp.maximum(i - nct, 0), 0))], list(x))


def _x_tile(x_refs, lay):
    if len(x_refs) == 1:
        return x_refs[0][...]
    in_ctx = pl.program_id(0) * ROW_TILE < _n_ctx_rows(lay)
    return jnp.where(in_ctx, x_refs[0][...], x_refs[1][...])


def _modulate_kernel(*refs, lay, n_x):
    x_refs, (sh_ref, sc_ref, h_ref) = refs[:n_x], refs[n_x:]
    seg = _tile_info(pl.program_id(0), lay)[0]
    sh = sh_ref[pl.ds(seg, 1), :]
    sc = sc_ref[pl.ds(seg, 1), :]
    h_ref[...] = (_x_tile(x_refs, lay) * (1.0 + sc) + sh).astype(h_ref.dtype)


def _modulate(x, mods, layer, lay):
    m = _n_rows(lay)
    x_specs, x_args = _x_specs(x, lay)
    return pl.pallas_call(
        functools.partial(_modulate_kernel, lay=lay, n_x=len(x_args)),
        grid=(m // ROW_TILE,),
        in_specs=x_specs + [_mod_spec(layer, 0), _mod_spec(layer, 1)],
        out_specs=pl.BlockSpec((ROW_TILE, D_MODEL), lambda i: (i, 0)),
        out_shape=jax.ShapeDtypeStruct((m, D_MODEL), BF16),
        compiler_params=_cparams(1),
        name="modulate",
    )(*x_args, mods, mods)


def _deepnorm_kernel(*refs, lay, with_router, n_x):
    x_refs, refs = refs[:n_x], refs[n_x:]
    if with_router:
        y_ref, g_ref, lng_ref, lnb_ref, sh_ref, sc_ref, wr_ref, xo_ref, h_ref, aff_ref = refs
    else:
        y_ref, g_ref, lng_ref, lnb_ref, sh_ref, sc_ref, xo_ref, h_ref = refs
    seg = _tile_info(pl.program_id(0), lay)[0]
    g = g_ref[pl.ds(seg, 1), :]
    v = DEEPNORM_ALPHA * _x_tile(x_refs, lay) + g * y_ref[...].astype(F32)
    mu = jnp.mean(v, axis=-1, keepdims=True)
    d = v - mu
    var = jnp.mean(d * d, axis=-1, keepdims=True)
    xn = d * lax.rsqrt(var + LN_EPS) * lng_ref[...] + lnb_ref[...]
    xo_ref[...] = xn
    sh = sh_ref[pl.ds(seg, 1), :]
    sc = sc_ref[pl.ds(seg, 1), :]
    hb = (xn * (1.0 + sc) + sh).astype(BF16)
    h_ref[...] = hb
    if with_router:
        logits = lax.dot_general(wr_ref[...].astype(BF16), hb, (((1,), (1,)), ((), ())),
                                 preferred_element_type=F32)
        mx = jnp.max(logits, axis=0, keepdims=True)
        e = jnp.exp(logits - mx)
        aff_ref[...] = e / jnp.sum(e, axis=0, keepdims=True)


def _deepnorm(x, y, mods, layer, gate_chunk, ln_g, ln_b, next_layer, next_chunk, lay, w_router_t=None):
    m = y.shape[0]
    row = pl.BlockSpec((ROW_TILE, D_MODEL), lambda i: (i, 0))
    vec = pl.BlockSpec((None, 1, D_MODEL), lambda i: (layer, 0, 0))
    x_specs, x_args = _x_specs(x, lay)
    in_specs = x_specs + [row, _mod_spec(layer, gate_chunk), vec, vec,
                          _mod_spec(next_layer, next_chunk), _mod_spec(next_layer, next_chunk + 1)]
    args = x_args + [y, mods, ln_g.reshape(DEPTH, 1, D_MODEL), ln_b.reshape(DEPTH, 1, D_MODEL), mods, mods]
    out_specs = [row, row]
    out_shape = [jax.ShapeDtypeStruct((m, D_MODEL), F32), jax.ShapeDtypeStruct((m, D_MODEL), BF16)]
    with_router = w_router_t is not None
    if with_router:
        in_specs.append(pl.BlockSpec((None, N_EXPERTS, D_MODEL), lambda i: (layer, 0, 0)))
        args.append(w_router_t)
        out_specs.append(pl.BlockSpec((N_EXPERTS, ROW_TILE), lambda i: (0, i)))
        out_shape.append(jax.ShapeDtypeStruct((N_EXPERTS, m), F32))
    return pl.pallas_call(
        functools.partial(_deepnorm_kernel, lay=lay, with_router=with_router, n_x=len(x_args)),
        grid=(m // ROW_TILE,),
        in_specs=in_specs, out_specs=out_specs, out_shape=out_shape,
        compiler_params=_cparams(1),
        name="deepnorm",
    )(*args)


def _final_norm_kernel(x_ref, y_ref, g_ref, lng_ref, lnb_ref, xo_ref, *, lay, tile0):
    seg = _tile_info(tile0 + pl.program_id(0), lay)[0]
    v = DEEPNORM_ALPHA * x_ref[...] + g_ref[pl.ds(seg, 1), :] * y_ref[...].astype(F32)
    mu = jnp.mean(v, axis=-1, keepdims=True)
    d = v - mu
    var = jnp.mean(d * d, axis=-1, keepdims=True)
    xo_ref[...] = d * lax.rsqrt(var + LN_EPS) * lng_ref[...] + lnb_ref[...]


def _final_norm(x, y, mods, layer, gate_chunk, ln_g, ln_b, lay, tile0, n_tiles):
    row_in = pl.BlockSpec((ROW_TILE, D_MODEL), lambda i: (tile0 + i, 0))
    vec = pl.BlockSpec((None, 1, D_MODEL), lambda i: (layer, 0, 0))
    return pl.pallas_call(
        functools.partial(_final_norm_kernel, lay=lay, tile0=tile0),
        grid=(n_tiles,),
        in_specs=[row_in, row_in, _mod_spec(layer, gate_chunk), vec, vec],
        out_specs=pl.BlockSpec((ROW_TILE, D_MODEL), lambda i: (i, 0)),
        out_shape=jax.ShapeDtypeStruct((n_tiles * ROW_TILE, D_MODEL), F32),
        compiler_params=_cparams(1),
        name="final_norm",
    )(x, y, mods, ln_g.reshape(DEPTH, 1, D_MODEL), ln_b.reshape(DEPTH, 1, D_MODEL))


POOL_HALO = 8


def _pool_kernel(cur_ref, prev_ref, nxt_ref, w_ref, sc_ref, o_ref, *, lay):
    _, pos, ntile, seq_len = _tile_info(pl.program_id(0), lay)
    t = pos * ROW_TILE + lax.broadcasted_iota(jnp.int32, (ROW_TILE, 1), 0)
    for g, w in enumerate(POOL_WINDOWS):
        cs = slice(g * POOL_GROUP, (g + 1) * POOL_GROUP)
        cur = cur_ref[:, cs].astype(F32)
        ext = _extend(prev_ref[:, cs].astype(F32), cur, nxt_ref[:, cs].astype(F32), pos, ntile)
        s, span = ext, 1
        while span < w:
            s = s + _rows_ahead(s, span)
            span *= 2
        lo = (w - 1) // 2
        win = _rows_ahead(s, POOL_HALO - lo)[:ROW_TILE]
        cnt = jnp.minimum(t + w // 2, seq_len - 1) - jnp.maximum(t - lo, 0) + 1
        pooled = win / cnt.astype(F32) - cur
        y = jnp.dot(pooled.astype(BF16), w_ref[g].astype(BF16), preferred_element_type=F32)
        o_ref[:, cs] = (y * sc_ref[:, cs]).astype(o_ref.dtype)


def _pool(u_pool, pool_w, pool_scale, layer, lay):
    m = u_pool.shape[0]
    return pl.pallas_call(
        functools.partial(_pool_kernel, lay=lay),
        grid=(m // ROW_TILE,),
        in_specs=_halo_specs(POOL_WIDTH, POOL_HALO, m) + [
            pl.BlockSpec((None, N_POOL, POOL_GROUP, POOL_GROUP), lambda i: (layer, 0, 0, 0)),
            pl.BlockSpec((None, 1, POOL_WIDTH), lambda i: (layer, 0, 0))],
        out_specs=pl.BlockSpec((ROW_TILE, POOL_WIDTH), lambda i: (i, 0)),
        out_shape=jax.ShapeDtypeStruct((m, POOL_WIDTH), BF16),
        compiler_params=_cparams(1),
        name="pool_mixer",
    )(u_pool, u_pool, u_pool, pool_w, pool_scale.reshape(DEPTH, 1, POOL_WIDTH))


CONF_HALO = 16


def _conformer_kernel(cur_ref, prev_ref, nxt_ref, w_ref, b_ref, g_ref, bb_ref, o_ref, *, lay):
    _, pos, ntile, _ = _tile_info(pl.program_id(0), lay)

    def glu(ref):
        return ref[:, :CONF_WIDTH].astype(F32) * jax.nn.sigmoid(ref[:, CONF_WIDTH:].astype(F32))

    ext = _extend(glu(prev_ref), glu(cur_ref), glu(nxt_ref), pos, ntile)
    base = CONF_HALO - CONF_KERNEL // 2
    acc = jnp.zeros((ROW_TILE, CONF_WIDTH), F32)
    for sub in range(SUBLANE):
        r = _rows_ahead(ext, sub)
        for k in range(CONF_KERNEL):
            o = base + k
            if o % SUBLANE == sub:
                al = o - sub
                acc = acc + r[al:al + ROW_TILE] * w_ref[k:k + 1, :]
    hcv = acc + b_ref[...]
    mu = jnp.mean(hcv, axis=-1, keepdims=True)
    d = hcv - mu
    var = jnp.mean(d * d, axis=-1, keepdims=True)
    o_ref[...] = _silu(d * lax.rsqrt(var + LN_EPS) * g_ref[...] + bb_ref[...]).astype(o_ref.dtype)


def _conformer(u_conf, dw_w, dw_b, ln_g, ln_b, layer, lay):
    m = u_conf.shape[0]
    vec = pl.BlockSpec((None, 1, CONF_WIDTH), lambda i: (layer, 0, 0))
    return pl.pallas_call(
        functools.partial(_conformer_kernel, lay=lay),
        grid=(m // ROW_TILE,),
        in_specs=_halo_specs(2 * CONF_WIDTH, CONF_HALO, m) + [
            pl.BlockSpec((None, CONF_KERNEL, CONF_WIDTH), lambda i: (layer, 0, 0)), vec, vec, vec],
        out_specs=pl.BlockSpec((ROW_TILE, CONF_WIDTH), lambda i: (i, 0)),
        out_shape=jax.ShapeDtypeStruct((m, CONF_WIDTH), BF16),
        compiler_params=_cparams(1),
        name="conformer_conv",
    )(u_conf, u_conf, u_conf, dw_w, dw_b.reshape(DEPTH, 1, CONF_WIDTH),
      ln_g.reshape(DEPTH, 1, CONF_WIDTH), ln_b.reshape(DEPTH, 1, CONF_WIDTH))


SSD_HALO = 8


def _ssd_pre_kernel(cur_ref, prev_ref, nxt_ref, w_ref, b_ref, dt_ref, dtb_ref, o_ref, dtt_ref, *, lay):
    _, pos, ntile, _ = _tile_info(pl.program_id(0), lay)
    ext = _extend(prev_ref[...].astype(F32), cur_ref[...].astype(F32), nxt_ref[...].astype(F32), pos, ntile)
    base = SSD_HALO - (SSD_CONV - 1) // 2
    acc = None
    for k in range(SSD_CONV):
        o = base + k
        sub = o % SUBLANE
        term = _rows_ahead(ext, sub)[o - sub:o - sub + ROW_TILE] * w_ref[k:k + 1, :]
        acc = term if acc is None else acc + term
    o_ref[...] = _silu(acc + b_ref[...]).astype(o_ref.dtype)
    z = dt_ref[...].astype(F32) + dtb_ref[...]
    sp = jnp.maximum(z, 0.0) + jnp.log1p(jnp.exp(-jnp.abs(z)))
    dtt_ref[...] = sp.T


def _ssd_pre(u_xbc, u_dt, conv_w, conv_b, dt_bias, layer, lay, col_tile=1024):
    m = u_xbc.shape[0]
    ncol = SSD_XBC // col_tile
    dtb = jnp.pad(dt_bias.reshape(DEPTH, 1, 2 * SSD_HEADS), ((0, 0), (0, 0), (0, LANE - 2 * SSD_HEADS)))

    def kern(cur_ref, prev_ref, nxt_ref, w_ref, b_ref, dt_ref, dtb_ref, o_ref, dtt_ref):
        _ssd_pre_kernel(cur_ref, prev_ref, nxt_ref, w_ref, b_ref, dt_ref, dtb_ref, o_ref, dtt_ref, lay=lay)

    per = ROW_TILE // SSD_HALO
    last = m // SSD_HALO - 1
    return pl.pallas_call(
        kern,
        grid=(m // ROW_TILE, ncol),
        in_specs=[pl.BlockSpec((ROW_TILE, col_tile), lambda i, j: (i, j)),
                  pl.BlockSpec((SSD_HALO, col_tile), lambda i, j: (jnp.maximum(i * per - 1, 0), j)),
                  pl.BlockSpec((SSD_HALO, col_tile), lambda i, j: (jnp.minimum((i + 1) * per, last), j)),
                  pl.BlockSpec((None, SSD_CONV, col_tile), lambda i, j: (layer, 0, j)),
                  pl.BlockSpec((None, 1, col_tile), lambda i, j: (layer, 0, j)),
                  pl.BlockSpec((ROW_TILE, LANE), lambda i, j: (i, 0)),
                  pl.BlockSpec((None, 1, LANE), lambda i, j: (layer, 0, 0))],
        out_specs=[pl.BlockSpec((ROW_TILE, col_tile), lambda i, j: (i, j)),
                   pl.BlockSpec((LANE, ROW_TILE), lambda i, j: (0, i))],
        out_shape=[jax.ShapeDtypeStruct((m, SSD_XBC), BF16), jax.ShapeDtypeStruct((LANE, m), F32)],
        compiler_params=_cparams(2),
        name="ssd_conv",
    )(u_xbc, u_xbc, u_xbc, conv_w, conv_b.reshape(DEPTH, 1, SSD_XBC), u_dt, dtb)


def _ssd_scan_kernel(*refs, direction, has_h0, want_state, n_chunks):
    it = iter(refs)
    x_ref, b_ref, c_ref, dtt_ref, alog_ref = next(it), next(it), next(it), next(it), next(it)
    h0_ref = next(it) if has_h0 else None
    if direction == 1:
        yf_ref, z_ref, dsk_ref, ng_ref = next(it), next(it), next(it), next(it)
    next(it)
    if want_state:
        next(it)
    o_ref = next(it)
    hfin_ref = next(it) if want_state else None
    h_ref = next(it)

    ci = pl.program_id(1)

    @pl.when(ci == 0)
    def _():
        if has_h0:
            h_ref[...] = h0_ref[...]
        else:
            h_ref[...] = jnp.zeros_like(h_ref)

    q = SSD_CHUNK
    row0 = direction * SSD_HEADS
    a_col = -jnp.exp(alog_ref[...])
    dtt = dtt_ref[...]
    a_dt = dtt * a_col
    ii = lax.broadcasted_iota(jnp.int32, (q, q), 0)
    jj = lax.broadcasted_iota(jnp.int32, (q, q), 1)
    incl = (jj <= ii) if direction == 0 else (jj >= ii)
    incl_f = jnp.where(incl, 1.0, 0.0).astype(F32)
    ident = jnp.where(ii == jj, 1.0, 0.0).astype(F32)
    nt = (((1,), (1,)), ((), ()))
    cum_row = lax.dot_general(a_dt, incl_f, nt, precision=HIGHEST, preferred_element_type=F32)
    cum_col = lax.dot_general(incl_f, a_dt, nt, precision=HIGHEST, preferred_element_type=F32)
    dt_col = lax.dot_general(ident, dtt, nt, precision=HIGHEST, preferred_element_type=F32)
    last = q - 1 if direction == 0 else 0
    tot_row = cum_col[last:last + 1, :]
    tot_col = cum_row[:, last:last + 1]
    dec_end_col = jnp.exp(tot_row - cum_col)
    dec_in_col = jnp.exp(cum_col)
    chunk_decay = jnp.exp(tot_col)

    def split(v):
        hi = v.astype(BF16)
        return hi, (v - hi.astype(F32)).astype(BF16)

    def spread(parts, expand):
        out = None
        for p in parts:
            t = jnp.dot(p, expand, preferred_element_type=F32)
            out = t if out is None else out + t
        return out

    dt_parts = split(dt_col)
    dec_end_parts = (dec_end_col.astype(BF16),)
    dec_in_parts = (dec_in_col.astype(BF16),)

    lane_head = lax.broadcasted_iota(jnp.int32, (LANE, SSD_GROUP_WIDTH), 1) // SSD_HEADDIM
    tab_row = lax.broadcasted_iota(jnp.int32, (LANE, SSD_GROUP_WIDTH), 0)
    heads_per_tile = LANE // SSD_HEADDIM
    tile_head = lax.broadcasted_iota(jnp.int32, (q, LANE), 1) // SSD_HEADDIM

    for g in range(SSD_GROUPS):
        gs = slice(g * SSD_GROUP_WIDTH, (g + 1) * SSD_GROUP_WIDTH)
        ns = slice(g * SSD_STATE, (g + 1) * SSD_STATE)
        head0 = row0 + g * SSD_GROUP_HEADS
        expand = jnp.where(tab_row == head0 + lane_head, 1.0, 0.0).astype(BF16)
        xg = x_ref[:, gs].astype(F32)
        bg = b_ref[:, ns].astype(BF16)
        cg = c_ref[:, ns].astype(BF16)
        xdt = xg * spread(dt_parts, expand)
        xdt_b = xdt.astype(BF16)
        xdt_end = (xdt * spread(dec_end_parts, expand)).astype(BF16)
        cb = lax.dot_general(cg, bg, nt, preferred_element_type=F32)
        y_tiles = []
        for t in range(SSD_GROUP_WIDTH // LANE):
            x_tile = xdt_b[:, t * LANE:(t + 1) * LANE]
            y_tile = None
            for k in range(heads_per_tile):
                hh = head0 + t * heads_per_tile + k
                seg = cum_col[:, hh:hh + 1] - cum_row[hh:hh + 1, :]
                gmat = (cb * jnp.exp(jnp.where(incl, seg, MASKED))).astype(BF16)
                part = jnp.dot(gmat, x_tile, preferred_element_type=F32)
                y_tile = part if y_tile is None else jnp.where(tile_head == k, part, y_tile)
            y_tiles.append(y_tile)
        y = jnp.concatenate(y_tiles, axis=1)
        hs = slice(g * SSD_GROUP_WIDTH, (g + 1) * SSD_GROUP_WIDTH)
        h_in = h_ref[hs, :]
        y_off = lax.dot_general(cg, h_in.astype(BF16), nt, preferred_element_type=F32)
        y = y + y_off * spread(dec_in_parts, expand)
        s_chunk = lax.dot_general(xdt_end, bg, (((0,), (0,)), ((), ())), preferred_element_type=F32)
        for k in range(SSD_GROUP_HEADS):
            hh = head0 + k
            rs = slice(g * SSD_GROUP_WIDTH + k * SSD_HEADDIM, g * SSD_GROUP_WIDTH + (k + 1) * SSD_HEADDIM)
            h_ref[rs, :] = (h_in[k * SSD_HEADDIM:(k + 1) * SSD_HEADDIM] * chunk_decay[hh:hh + 1, :]
                            + s_chunk[k * SSD_HEADDIM:(k + 1) * SSD_HEADDIM])
        if direction == 0:
            o_ref[:, gs] = y
        else:
            v = (yf_ref[:, gs] + y + xg * dsk_ref[:, gs]) * _silu(z_ref[:, gs].astype(F32))
            v = v * lax.rsqrt(jnp.mean(v * v, axis=-1, keepdims=True) + LN_EPS)
            o_ref[:, gs] = (v * ng_ref[:, gs]).astype(o_ref.dtype)

    if want_state:
        @pl.when(ci == n_chunks - 1)
        def _():
            hfin_ref[...] = h_ref[...]


def _ssd_scan(xbc, dtt, a_log, layer, direction, *, row0, n_seq, seq_len, out_buf, n_rows,
              h0=None, want_state=False, state_buf=None, yf=None, z=None, d_skip=None, norm_g=None):
    q = SSD_CHUNK
    nc = seq_len // q
    rb0 = row0 // q

    def rb(s, c):
        return rb0 + s * nc + (c if direction == 0 else nc - 1 - c)

    bcol = SSD_INNER // (SSD_GROUPS * SSD_STATE)
    alog = jnp.pad(a_log.reshape(DEPTH, 2 * SSD_HEADS, 1), ((0, 0), (0, LANE - 2 * SSD_HEADS), (0, 0)))
    in_specs = [pl.BlockSpec((q, SSD_INNER), lambda s, c: (rb(s, c), 0)),
                pl.BlockSpec((q, SSD_GROUPS * SSD_STATE), lambda s, c: (rb(s, c), bcol)),
                pl.BlockSpec((q, SSD_GROUPS * SSD_STATE), lambda s, c: (rb(s, c), bcol + 1)),
                pl.BlockSpec((LANE, q), lambda s, c: (0, rb(s, c))),
                pl.BlockSpec((None, LANE, 1), lambda s, c: (layer, 0, 0))]
    args = [xbc, xbc, xbc, dtt, alog]
    if h0 is not None:
        in_specs.append(pl.BlockSpec((None, None, None, SSD_INNER, SSD_STATE),
                                     lambda s, c: (s, layer, direction, 0, 0)))
        args.append(h0)
    row_spec = pl.BlockSpec((q, SSD_INNER), lambda s, c: (rb(s, c), 0))
    if direction == 1:
        vec = pl.BlockSpec((None, 1, SSD_INNER), lambda s, c: (layer, 0, 0))
        in_specs += [row_spec, row_spec, vec, vec]
        args += [yf, z, d_skip, norm_g]
    out_dtype = F32 if direction == 0 else BF16
    out_shape = [jax.ShapeDtypeStruct((n_rows, SSD_INNER), out_dtype)]
    out_specs = [row_spec]
    aliases = {}
    in_specs.append(pl.BlockSpec(memory_space=pl.ANY))
    if out_buf is None:
        args.append(jnp.zeros((SUBLANE, LANE), F32))
    else:
        aliases = {len(args): 0}
        args.append(out_buf)
    if want_state:
        out_shape.append(jax.ShapeDtypeStruct((n_seq, DEPTH, 2, SSD_INNER, SSD_STATE), F32))
        out_specs.append(pl.BlockSpec((None, None, None, SSD_INNER, SSD_STATE),
                                      lambda s, c: (s, layer, direction, 0, 0)))
        in_specs.append(pl.BlockSpec(memory_space=pl.ANY))
        if state_buf is None:
            args.append(jnp.zeros((SUBLANE, LANE), F32))
        else:
            aliases[len(args)] = 1
            args.append(state_buf)
    kern = functools.partial(_ssd_scan_kernel, direction=direction, has_h0=h0 is not None,
                             want_state=want_state, n_chunks=nc)
    res = pl.pallas_call(
        kern,
        grid=(n_seq, nc),
        in_specs=in_specs, out_specs=out_specs, out_shape=out_shape,
        scratch_shapes=[pltpu.VMEM((SSD_INNER, SSD_STATE), F32)],
        input_output_aliases=aliases,
        compiler_params=_cparams(2),
        name=f"ssd_scan_d{direction}",
    )(*args)
    return res if want_state else (res[0], None)


def _ssd(u_z, xbc, dtt, a_log, d_skip, norm_g, state, state_buf, layer, lay):
    m = u_z.shape[0]
    groups = [dict(row0=0, n_seq=lay.n_ctx, seq_len=lay.ctx_len, h0=None, want_state=True),
              dict(row0=_n_ctx_rows(lay), n_seq=lay.n_lat, seq_len=lay.lat_len, h0=state, want_state=False)]
    dsk = jnp.repeat(d_skip, SSD_HEADDIM, axis=-1).reshape(DEPTH, 1, SSD_INNER)
    ng = norm_g.reshape(DEPTH, 1, SSD_INNER)
    yf, y = None, None
    for grp in groups:
        yf, sb = _ssd_scan(xbc, dtt, a_log, layer, 0, out_buf=yf, n_rows=m,
                           state_buf=state_buf if grp["want_state"] else None, **grp)
        state_buf = sb if grp["want_state"] else state_buf
    for grp in groups:
        y, sb = _ssd_scan(xbc, dtt, a_log, layer, 1, out_buf=y, n_rows=m, yf=yf, z=u_z, d_skip=dsk, norm_g=ng,
                          state_buf=state_buf if grp["want_state"] else None, **grp)
        state_buf = sb if grp["want_state"] else state_buf
    return y, state_buf


def _head_masks(rows):
    lane = lax.broadcasted_iota(jnp.int32, (rows, LANE), 1)
    return [lane // NA_HEADDIM == j for j in range(LANE // NA_HEADDIM)]


def _dense_attn_kernel(q_ref, k_ref, v_ref, kbuf_ref, vbuf_ref, o_ref, knew_ref, vnew_ref):
    del kbuf_ref, vbuf_ref
    knew_ref[...] = k_ref[...].astype(knew_ref.dtype)
    vnew_ref[...] = v_ref[...].astype(vnew_ref.dtype)
    rows = q_ref.shape[0]
    masks = _head_masks(rows)
    scale = NA_HEADDIM ** -0.5
    nt = (((1,), (1,)), ((), ()))

    def pair(hp, carry):
        cs = pl.ds(pl.multiple_of(hp * LANE, LANE), LANE)
        q = q_ref[:, cs].astype(BF16)
        k = k_ref[:, cs].astype(BF16)
        v = v_ref[:, cs].astype(BF16)
        q2 = jnp.concatenate([jnp.where(msk, q, jnp.zeros_like(q)) for msk in masks], axis=0)
        s = lax.dot_general(q2, k, nt, preferred_element_type=F32) * scale
        e = jnp.exp(s - jnp.max(s, axis=-1, keepdims=True))
        o2 = jnp.dot(e.astype(BF16), v, preferred_element_type=F32) / jnp.sum(e, axis=-1, keepdims=True)
        acc = o2[:rows]
        for j in range(1, len(masks)):
            acc = jnp.where(masks[j], o2[j * rows:(j + 1) * rows], acc)
        o_ref[:, cs] = acc.astype(o_ref.dtype)
        return carry

    lax.fori_loop(0, NA_WIDTH // LANE, pair, 0, unroll=2)


def _dense_attention(u_qkv, kv_bufs, layer, lay):
    m = u_qkv.shape[0]
    L = lay.ctx_len
    cache_sds = jax.ShapeDtypeStruct((lay.n_ctx, DEPTH, L, NA_WIDTH), F32)
    cache_spec = pl.BlockSpec((None, None, L, NA_WIDTH), lambda s: (s, layer, 0, 0))
    if kv_bufs is None:
        kv_args = [jnp.zeros((SUBLANE, LANE), F32)] * 2
        aliases = {}
    else:
        kv_args = list(kv_bufs)
        aliases = {3: 1, 4: 2}
    y, knew, vnew = pl.pallas_call(
        _dense_attn_kernel,
        grid=(lay.n_ctx,),
        in_specs=[pl.BlockSpec((L, NA_WIDTH), lambda s: (s, 0)),
                  pl.BlockSpec((L, NA_WIDTH), lambda s: (s, 1)),
                  pl.BlockSpec((L, NA_WIDTH), lambda s: (s, 2)),
                  pl.BlockSpec(memory_space=pl.ANY), pl.BlockSpec(memory_space=pl.ANY)],
        out_specs=[pl.BlockSpec((L, NA_WIDTH), lambda s: (s, 0)), cache_spec, cache_spec],
        out_shape=[jax.ShapeDtypeStruct((m, NA_WIDTH), BF16), cache_sds, cache_sds],
        input_output_aliases=aliases,
        compiler_params=_cparams(1),
        name="dense_attention",
    )(u_qkv, u_qkv, u_qkv, *kv_args)
    return y, (knew, vnew)


NA_QROWS = 4
NA_BIAS_PAIRS = 2 * NA_ROWS - 2


def _na_kernel(rpb_ref, q_ref, k_ref, v_ref, kc_ref, vc_ref, alias_ref, o_ref, tb_ref, *, n_grid_rows):
    del alias_ref
    hp = pl.program_id(1)
    rb = pl.program_id(2)
    n_dc = 2 * NA_COLS - 1
    heads_per_tile = LANE // NA_HEADDIM

    @pl.when(rb == 0)
    def _():
        qi = lax.broadcasted_iota(jnp.int32, (GRID_W, LANE), 0)
        ln = lax.broadcasted_iota(jnp.int32, (GRID_W, LANE), 1)
        kc = ln % GRID_W
        upper = ln >= GRID_W
        cstart = jnp.clip(qi - NA_COLS // 2, 0, GRID_W - NA_COLS)
        valid = (kc >= cstart) & (kc < cstart + NA_COLS)
        rel = kc - qi + NA_COLS - 1
        for j in range(heads_per_tile):
            head = hp * heads_per_tile + j
            for d in range(NA_BIAS_PAIRS):
                t = jnp.full((GRID_W, LANE), MASKED, F32)
                for dc in range(n_dc):
                    lo = rpb_ref[(head * (2 * NA_ROWS - 1) + d) * n_dc + dc]
                    hi = rpb_ref[(head * (2 * NA_ROWS - 1) + d + 1) * n_dc + dc]
                    t = jnp.where(valid & (rel == dc), jnp.where(upper, hi, lo), t)
                tb_ref[j, d] = t

    masks = _head_masks(GRID_W)
    scale = NA_HEADDIM ** -0.5
    nt = (((1,), (1,)), ((), ()))
    kctx = kc_ref[...].astype(BF16)
    vctx = vc_ref[...].astype(BF16)
    unit = heads_per_tile * GRID_W
    q2s, s_nbs, vws = [], [], []
    for rr in range(NA_QROWS):
        r = rb * NA_QROWS + rr
        rs = jnp.clip(r - NA_ROWS // 2, 0, n_grid_rows - NA_ROWS)
        q = q_ref[rr * GRID_W:(rr + 1) * GRID_W, :].astype(BF16)
        win = pl.ds(pl.multiple_of(rs * GRID_W, GRID_W), NA_ROWS * GRID_W)
        kw = k_ref[win, :].astype(BF16)
        vws.append(v_ref[win, :].astype(BF16))
        q2 = jnp.concatenate([jnp.where(msk, q, jnp.zeros_like(q)) for msk in masks], axis=0)
        d0 = rs - r + NA_ROWS - 1
        bias = jnp.concatenate(
            [jnp.concatenate([tb_ref[j, d0 + 2 * pp] for pp in range(NA_ROWS // 2)], axis=1)
             for j in range(heads_per_tile)], axis=0)
        q2s.append(q2)
        s_nbs.append(lax.dot_general(q2, kw, nt, preferred_element_type=F32) * scale + bias)
    s_nb = jnp.concatenate(s_nbs, axis=0)
    s_cx = lax.dot_general(jnp.concatenate(q2s, axis=0), kctx, nt, preferred_element_type=F32) * scale
    mx = jnp.maximum(jnp.max(s_nb, axis=-1, keepdims=True), jnp.max(s_cx, axis=-1, keepdims=True))
    e_nb = jnp.exp(s_nb - mx)
    e_cx = jnp.exp(s_cx - mx)
    den = jnp.sum(e_nb, axis=-1, keepdims=True) + jnp.sum(e_cx, axis=-1, keepdims=True)
    e_nb = e_nb.astype(BF16)
    o_cx = jnp.dot(e_cx.astype(BF16), vctx, preferred_element_type=F32)
    for rr in range(NA_QROWS):
        us = slice(rr * unit, (rr + 1) * unit)
        o2 = (jnp.dot(e_nb[us], vws[rr], preferred_element_type=F32) + o_cx[us]) / den[us]
        acc = o2[:GRID_W]
        for j in range(1, heads_per_tile):
            acc = jnp.where(masks[j], o2[j * GRID_W:(j + 1) * GRID_W], acc)
        o_ref[rr * GRID_W:(rr + 1) * GRID_W, :] = acc.astype(o_ref.dtype)


def _neighborhood_attention(u_qkv, y_buf, cache_k, cache_v, rpb, layer, lay):
    m = u_qkv.shape[0]
    L = lay.lat_len
    n_grid_rows = L // GRID_W
    assert n_grid_rows >= NA_ROWS and n_grid_rows % NA_QROWS == 0
    qrows = NA_QROWS * GRID_W
    q0 = _n_ctx_rows(lay) // qrows
    s0 = _n_ctx_rows(lay) // L
    assert _n_ctx_rows(lay) % L == 0
    n_pairs = NA_WIDTH // LANE
    past = cache_k.shape[2]
    return pl.pallas_call(
        functools.partial(_na_kernel, n_grid_rows=n_grid_rows),
        grid=(lay.n_lat, n_pairs, n_grid_rows // NA_QROWS),
        in_specs=[pl.BlockSpec(memory_space=pltpu.SMEM),
                  pl.BlockSpec((qrows, LANE), lambda b, hp, rb: (q0 + b * (L // qrows) + rb, hp)),
                  pl.BlockSpec((L, LANE), lambda b, hp, rb: (s0 + b, n_pairs + hp)),
                  pl.BlockSpec((L, LANE), lambda b, hp, rb: (s0 + b, 2 * n_pairs + hp)),
                  pl.BlockSpec((None, None, past, LANE), lambda b, hp, rb: (b, layer, 0, hp)),
                  pl.BlockSpec((None, None, past, LANE), lambda b, hp, rb: (b, layer, 0, hp)),
                  pl.BlockSpec(memory_space=pl.ANY)],
        out_specs=pl.BlockSpec((qrows, LANE), lambda b, hp, rb: (q0 + b * (L // qrows) + rb, hp)),
        out_shape=jax.ShapeDtypeStruct((m, NA_WIDTH), BF16),
        scratch_shapes=[pltpu.VMEM((LANE // NA_HEADDIM, NA_BIAS_PAIRS, GRID_W, LANE), F32)],
        input_output_aliases={6: 0},
        compiler_params=_cparams(3),
        name="neighborhood_attention",
    )(rpb[layer].reshape(-1), u_qkv, u_qkv, u_qkv, cache_k, cache_v, y_buf)


def _route_kernel(aff_ref, p_ref, g_ref, aff_s_ref, sel_ref, rank_ref, *, cap, e_blk, s_blk):
    eb = pl.program_id(1)
    t = aff_ref.shape[1] // s_blk
    n_rows = s_blk * N_EXPERTS

    @pl.when(eb == 0)
    def _():
        aff = jnp.concatenate([aff_ref[:, j * t:(j + 1) * t] for j in range(s_blk)], axis=0)
        aff_s_ref[...] = aff
        bits = pltpu.bitcast(aff, jnp.int32)
        capf = jnp.float32(cap)

        def bisect(_, lohi):
            lo, hi = lohi
            mid = lo + ((hi - lo) >> 1)
            cnt = jnp.sum(jnp.where(bits >= mid, 1.0, 0.0), axis=1, keepdims=True)
            ge = cnt >= capf
            return jnp.where(ge, mid, lo), jnp.where(ge, hi, mid)

        lo0 = jnp.zeros((n_rows, 1), jnp.int32)
        hi0 = jnp.full((n_rows, 1), 0x7F800000, jnp.int32)
        thr, _ = lax.fori_loop(0, 31, bisect, (lo0, hi0))
        gt = bits > thr
        eq = bits == thr
        n_gt = jnp.sum(jnp.where(gt, 1.0, 0.0), axis=1, keepdims=True)
        before = (lax.broadcasted_iota(jnp.int32, (t, t), 0) < lax.broadcasted_iota(jnp.int32, (t, t), 1))
        before = jnp.where(before, 1.0, 0.0).astype(BF16)
        eq_before = jnp.dot(jnp.where(eq, 1.0, 0.0).astype(BF16), before, preferred_element_type=F32)
        sel = gt | (eq & (eq_before < capf - n_gt))
        self_ = jnp.where(sel, 1.0, 0.0)
        sel_ref[...] = self_
        rank_ref[...] = jnp.dot(self_.astype(BF16), before, preferred_element_type=F32)

    slot = lax.broadcasted_iota(jnp.int32, (cap, t), 0).astype(F32)
    for j in range(s_blk):
        for k in range(e_blk):
            row = j * N_EXPERTS + eb * e_blk + k
            rank = rank_ref[pl.ds(row, 1), :]
            sel = sel_ref[pl.ds(row, 1), :]
            onehot = (slot == rank) & (sel > 0.5)
            p_ref[j, k * cap:(k + 1) * cap, :] = jnp.where(onehot, 1.0, 0.0).astype(p_ref.dtype)
            g_ref[j, k * cap:(k + 1) * cap, :] = jnp.sum(jnp.where(onehot, aff_s_ref[pl.ds(row, 1), :], 0.0),
                                                         axis=1, keepdims=True)


def _route(aff_t, *, row0, n_seq, seq_len, e_blk, s_blk):
    cap = EC_CAPACITY * seq_len // N_EXPERTS
    assert row0 % (s_blk * seq_len) == 0 and n_seq % s_blk == 0
    s0 = row0 // (s_blk * seq_len)
    rows = s_blk * N_EXPERTS
    return pl.pallas_call(
        functools.partial(_route_kernel, cap=cap, e_blk=e_blk, s_blk=s_blk),
        grid=(n_seq // s_blk, N_EXPERTS // e_blk),
        in_specs=[pl.BlockSpec((N_EXPERTS, s_blk * seq_len), lambda s, e: (0, s0 + s))],
        out_specs=[pl.BlockSpec((s_blk, e_blk * cap, seq_len), lambda s, e: (s, e, 0)),
                   pl.BlockSpec((s_blk, e_blk * cap, 1), lambda s, e: (s, e, 0))],
        out_shape=[jax.ShapeDtypeStruct((n_seq, N_EXPERTS * cap, seq_len), BF16),
                   jax.ShapeDtypeStruct((n_seq, N_EXPERTS * cap, 1), F32)],
        scratch_shapes=[pltpu.VMEM((rows, seq_len), F32)] * 3,
        compiler_params=_cparams(2),
        name="route",
    )(aff_t)


def _gather_kernel(p_ref, h_ref, alias_ref, o_ref, *, cap, n_e):
    del alias_ref
    res = jnp.dot(p_ref[...], h_ref[...], preferred_element_type=F32).astype(o_ref.dtype)
    for k in range(n_e):
        o_ref[k] = res[k * cap:(k + 1) * cap]


def _gather(p, h, xe_buf, *, row0, n_seq, seq_len, slot0, n_slots, e_blk, td):
    cap = EC_CAPACITY * seq_len // N_EXPERTS
    s0 = row0 // seq_len
    c0 = slot0 // cap
    assert slot0 % cap == 0
    in_specs = [pl.BlockSpec((None, e_blk * cap, seq_len), lambda s, eb, dt: (s, eb, 0)),
                pl.BlockSpec((seq_len, td), lambda s, eb, dt: (s0 + s, dt)),
                pl.BlockSpec(memory_space=pl.ANY)]
    aliases = {}
    if xe_buf is None:
        xe_buf = jnp.zeros((SUBLANE, LANE), F32)
    else:
        aliases = {2: 0}
    return pl.pallas_call(
        functools.partial(_gather_kernel, cap=cap, n_e=e_blk),
        grid=(n_seq, N_EXPERTS // e_blk, D_MODEL // td),
        in_specs=in_specs,
        out_specs=pl.BlockSpec((e_blk, cap, td), lambda s, eb, dt: (eb, c0 + s, dt)),
        out_shape=jax.ShapeDtypeStruct((N_EXPERTS, n_slots, D_MODEL), BF16),
        input_output_aliases=aliases,
        compiler_params=_cparams(3),
        name="expert_gather",
    )(p, h, xe_buf)


def _scatter_kernel(p_ref, y_ref, alias_ref, o_ref, *, cap, n_e):
    del alias_ref
    eb = pl.program_id(2)
    ye = jnp.concatenate([y_ref[k] for k in range(n_e)], axis=0)
    res = lax.dot_general(p_ref[...], ye, (((0,), (0,)), ((), ())), preferred_element_type=F32)

    @pl.when(eb == 0)
    def _():
        o_ref[...] = res

    @pl.when(eb > 0)
    def _():
        o_ref[...] += res


def _scatter(p, ye, out_buf, *, row0, n_seq, seq_len, slot0, n_rows, e_blk, td):
    cap = EC_CAPACITY * seq_len // N_EXPERTS
    s0 = row0 // seq_len
    c0 = slot0 // cap
    in_specs = [pl.BlockSpec((None, e_blk * cap, seq_len), lambda s, dt, eb: (s, eb, 0)),
                pl.BlockSpec((e_blk, cap, td), lambda s, dt, eb: (eb, c0 + s, dt)),
                pl.BlockSpec(memory_space=pl.ANY)]
    aliases = {}
    if out_buf is None:
        out_buf = jnp.zeros((SUBLANE, LANE), F32)
    else:
        aliases = {2: 0}
    return pl.pallas_call(
        functools.partial(_scatter_kernel, cap=cap, n_e=e_blk),
        grid=(n_seq, D_MODEL // td, N_EXPERTS // e_blk),
        in_specs=in_specs,
        out_specs=pl.BlockSpec((seq_len, td), lambda s, dt, eb: (s0 + s, dt)),
        out_shape=jax.ShapeDtypeStruct((n_rows, D_MODEL), F32),
        input_output_aliases=aliases,
        compiler_params=_cparams(3),
        name="expert_scatter",
    )(p, ye, out_buf)


def _expert_ffn(h2, aff_t, w_e_gate, w_e_up, w_e_down, layer, lay):
    m = h2.shape[0]
    cap_c = EC_CAPACITY * lay.ctx_len // N_EXPERTS
    cap_l = EC_CAPACITY * lay.lat_len // N_EXPERTS
    slots_c = lay.n_ctx * cap_c
    n_slots = slots_c + lay.n_lat * cap_l
    ctx = dict(row0=0, n_seq=lay.n_ctx, seq_len=lay.ctx_len)
    lat = dict(row0=_n_ctx_rows(lay), n_seq=lay.n_lat, seq_len=lay.lat_len)
    s_blk_c = max(d for d in (8, 4, 2, 1) if lay.n_ctx % d == 0)
    p_c, g_c = _route(aff_t, e_blk=N_EXPERTS, s_blk=s_blk_c, **ctx)
    p_l, g_l = _route(aff_t, e_blk=1, s_blk=1, **lat)
    xe = _gather(p_c, h2, None, slot0=0, n_slots=n_slots, e_blk=N_EXPERTS, td=D_MODEL, **ctx)
    xe = _gather(p_l, h2, xe, slot0=slots_c, n_slots=n_slots, e_blk=4, td=2048, **lat)
    ge = jnp.concatenate([
        jnp.swapaxes(g_c.reshape(lay.n_ctx, N_EXPERTS, cap_c), 0, 1).reshape(N_EXPERTS, slots_c, 1),
        jnp.swapaxes(g_l.reshape(lay.n_lat, N_EXPERTS, cap_l), 0, 1).reshape(N_EXPERTS, lay.n_lat * cap_l, 1)],
        axis=1)
    hid = _matmul(xe, [w_e_gate, w_e_up], lead=(layer,), batched_w=True, col_start=0,
                  n_cols=EXPERT_FF, tm=n_slots, tn=256, out_dtype=BF16, epilogue="swiglu", name="expert_up")
    ye = _matmul(hid, [w_e_down], lead=(layer,), batched_w=True, col_start=0, n_cols=D_MODEL,
                 tm=n_slots, tn=1024, out_dtype=BF16, scale=ge, name="expert_down")
    ffn = _scatter(p_c, ye, None, slot0=0, n_rows=m, e_blk=N_EXPERTS, td=D_MODEL, **ctx)
    ffn = _scatter(p_l, ye, ffn, slot0=slots_c, n_rows=m, e_blk=4, td=1024, **lat)
    return ffn


BIG_TM = 1024
BIG_TN = 512


def _modulation_table(c, c_ctx, w_mod, b_mod):
    cond = jnp.concatenate([c_ctx[None], c, jnp.zeros((MOD_ROWS - 1 - c.shape[0], D_MODEL), F32)], axis=0)
    return jnp.concatenate([
        _matmul(cond[None], [w_mod], lead=(l,), batched_w=False, col_start=0, n_cols=6 * D_MODEL,
                tm=MOD_ROWS, tn=1024, out_dtype=F32, prologue="silu",
                bias=b_mod.reshape(DEPTH, 1, 6 * D_MODEL), bias_lead=(l,), name="mod")
        for l in range(DEPTH)], axis=0)


def kernel(x_prompt, x_sample, cache_k, cache_v, state_ssm, c, c_ctx, w_mod, b_mod, w_in, pool_w, pool_scale, ssd_conv_w, ssd_conv_b, ssd_dt_bias, ssd_a_log, ssd_d, ssd_norm_g, na_rpb, conf_dw_w, conf_dw_b, conf_ln_g, conf_ln_b, w_br_pool, w_br_ssd, w_br_na, w_br_conf, w_out, ln1_g, ln1_b, w_router, w_e_gate, w_e_up, w_e_down, ln2_g, ln2_b):
    lay = FULL
    n_ctx_rows = _n_ctx_rows(lay)
    x = (x_prompt.reshape(n_ctx_rows, D_MODEL), x_sample.reshape(lay.n_lat * lay.lat_len, D_MODEL))
    mods = _modulation_table(c, c_ctx, w_mod, b_mod)
    w_router_t = jnp.swapaxes(w_router, 1, 2)
    w_in_t = jnp.swapaxes(w_in, 1, 2)
    cache_k2 = cache_k.reshape(DEC_BATCH, DEPTH, PAST_LEN, NA_WIDTH)
    cache_v2 = cache_v.reshape(DEC_BATCH, DEPTH, PAST_LEN, NA_WIDTH)
    state2 = state_ssm.reshape(DEC_BATCH, DEPTH, 2, SSD_INNER, SSD_STATE)

    h = _modulate(x, mods, 0, lay)
    kv_bufs, state_buf = None, None
    for l in range(DEPTH):
        def in_proj(sec, out_dtype=F32, epilogue=None, tn=BIG_TN, width=None):
            return _matmul(h[None], [w_in_t], lead=(l,), batched_w=False, col_start=IN_STARTS[sec],
                           n_cols=width or IN_SIZES[sec], tm=BIG_TM, tn=tn, out_dtype=out_dtype,
                           epilogue=epilogue, w_transposed=True, name=f"in_proj{sec}")[0]

        u_pool = in_proj(0)
        u_z = in_proj(1, out_dtype=BF16)
        u_xbc = in_proj(2)
        u_dt = in_proj(3, tn=LANE, width=LANE)
        u_qkv = in_proj(4, out_dtype=BF16)
        u_conf = in_proj(5, out_dtype=BF16)
        gates = in_proj(6, out_dtype=BF16, epilogue="sigmoid")

        y_pool = _pool(u_pool, pool_w, pool_scale, l, lay)
        xbc, dtt = _ssd_pre(u_xbc, u_dt, ssd_conv_w, ssd_conv_b, ssd_dt_bias, l, lay)
        y_ssd, state_buf = _ssd(u_z, xbc, dtt, ssd_a_log, ssd_d, ssd_norm_g, state2, state_buf, l, lay)
        y_na, kv_bufs = _dense_attention(u_qkv, kv_bufs, l, lay)
        y_na = _neighborhood_attention(u_qkv, y_na, cache_k2, cache_v2, na_rpb, l, lay)
        y_conf = _conformer(u_conf, conf_dw_w, conf_dw_b, conf_ln_g, conf_ln_b, l, lay)

        merged = _merge([y_pool, y_ssd, y_na, y_conf], [w_br_pool, w_br_ssd, w_br_na, w_br_conf], l, gates)
        mix = _matmul(merged[None], [w_out], lead=(l,), batched_w=False, col_start=0, n_cols=D_MODEL,
                      tm=BIG_TM, tn=BIG_TN, out_dtype=BF16, name="out_proj")[0]
        x, h2, aff_t = _deepnorm(x, mix, mods, l, 2, ln1_g, ln1_b, l, 3, lay, w_router_t=w_router_t)
        ffn = _expert_ffn(h2, aff_t, w_e_gate, w_e_up, w_e_down, l, lay)
        if l + 1 < DEPTH:
            x, h = _deepnorm(x, ffn, mods, l, 5, ln2_g, ln2_b, l + 1, 0, lay)

    last = DEPTH - 1
    n_ctx_tiles = n_ctx_rows // ROW_TILE
    y_prompt = _final_norm(x, ffn, mods, last, 5, ln2_g, ln2_b, lay, 0, n_ctx_tiles)
    y_sample = _final_norm(x, ffn, mods, last, 5, ln2_g, ln2_b, lay, n_ctx_tiles,
                           lay.n_lat * lay.lat_len // ROW_TILE)
    new_cache_k = kv_bufs[0].reshape(BATCH, DEPTH, SEQ, NA_HEADS, NA_HEADDIM)
    new_cache_v = kv_bufs[1].reshape(BATCH, DEPTH, SEQ, NA_HEADS, NA_HEADDIM)
    new_state = state_buf.reshape(BATCH, DEPTH, 2, SSD_HEADS, SSD_HEADDIM, SSD_STATE)
    return (y_prompt.reshape(BATCH, SEQ, D_MODEL), y_sample.reshape(DEC_BATCH, DEC_SEQ, D_MODEL),
            new_cache_k, new_cache_v, new_state)
```
